```python
import jax
import jax.numpy as jnp
from jax import lax
import numpy as np


D_MODEL = 1024
BATCH = 32
SEQ = 2048
DEPTH = 1

PLE_DIM = 256
GDN_HEADS = 4
GDN_HEAD_DIM = 128
GDN_WIDTH = GDN_HEADS * GDN_HEAD_DIM
GDN_CONV = 4
GDN_CHUNK = 64
SB_HEADS = 8
SB_HEAD_DIM = 64
SB_WIDTH = SB_HEADS * SB_HEAD_DIM
SB_BLOCK = 128
D_MIX = GDN_WIDTH + SB_WIDTH
IN_COLS = 4 * GDN_WIDTH + 2 * GDN_HEADS + 3 * SB_WIDTH
N_GROUPS = 8
EXPERTS_PER_GROUP = 8
N_EXPERTS = N_GROUPS * EXPERTS_PER_GROUP
TOP_K = 2
D_EXPERT = D_MODEL // 4
MOE_BLOCK = 128
EPS = 1e-6

kernel_name = 'hybrid_gdn_stickbreak_hmoe_block'


def rms_norm(x, w):
    xf = x.astype(jnp.float32)
    y = xf * lax.rsqrt(jnp.mean(xf * xf, axis=-1, keepdims=True) + EPS)
    return (y * w.astype(jnp.float32)).astype(x.dtype)


def l2_normalize(x):
    return x * lax.rsqrt(jnp.sum(x * x, axis=-1, keepdims=True) + EPS)


def causal_depthwise_conv_silu(u, w):
    k_width, channels = w.shape
    y = lax.conv_general_dilated(
        u, w[:, None, :].astype(u.dtype), window_strides=(1,),
        padding=[(k_width - 1, 0)], dimension_numbers=('NWC', 'WIO', 'NWC'),
        feature_group_count=channels)
    return jax.nn.silu(y)


def gated_deltanet(qkv, z, b, a, conv_w, a_log, dt_bias, norm_w):
    bsz, seq, _ = qkv.shape
    H, Dh, C = GDN_HEADS, GDN_HEAD_DIM, GDN_CHUNK
    n_chunks = seq // C
    qkv = causal_depthwise_conv_silu(qkv, conv_w).astype(jnp.float32)
    q, k, v = jnp.split(qkv, 3, axis=-1)
    q = l2_normalize(q.reshape(bsz, seq, H, Dh)) * (Dh ** -0.5)
    k = l2_normalize(k.reshape(bsz, seq, H, Dh))
    v = v.reshape(bsz, seq, H, Dh)
    beta = jax.nn.sigmoid(b.astype(jnp.float32))
    g = -jnp.exp(a_log.astype(jnp.float32)) * jax.nn.softplus(
        a.astype(jnp.float32) + dt_bias.astype(jnp.float32))

    def to_chunks(t):
        t = t.reshape((bsz, n_chunks, C, H) + t.shape[3:])
        return jnp.moveaxis(t, 3, 1)

    q, k, v, beta, g = (to_chunks(t) for t in (q, k, v, beta, g))
    gc = jnp.cumsum(g, axis=-1)
    incl = jnp.tril(jnp.ones((C, C), dtype=bool))
    strict = jnp.tril(jnp.ones((C, C), dtype=bool), k=-1)
    decay = jnp.exp(jnp.where(incl, gc[..., :, None] - gc[..., None, :], -jnp.inf))
    kk = jnp.einsum('bhnid,bhnjd->bhnij', k, k)
    lower = jnp.where(strict, beta[..., None] * kk * decay, 0.0)
    eye = jnp.eye(C, dtype=jnp.float32)
    rhs = jnp.concatenate([v * beta[..., None], k * (beta * jnp.exp(gc))[..., None]], axis=-1)
    sol = lax.linalg.triangular_solve(lower + eye, rhs, left_side=True, lower=True,
                                      unit_diagonal=True)
    u, w = sol[..., :Dh], sol[..., Dh:]
    qk = jnp.einsum('bhnid,bhnjd->bhnij', q, k) * decay
    xs = tuple(jnp.moveaxis(t, 2, 0) for t in (q, k, u, w, gc, qk))

    def chunk_step(state, inp):
        q_c, k_c, u_c, w_c, gc_c, qk_c = inp
        v_new = u_c - jnp.einsum('bhck,bhkv->bhcv', w_c, state)
        o_c = (jnp.einsum('bhck,bhkv->bhcv', q_c * jnp.exp(gc_c)[..., None], state)
               + jnp.einsum('bhcs,bhsv->bhcv', qk_c, v_new))
        g_last = gc_c[..., -1:]
        state = (state * jnp.exp(g_last)[..., None]
                 + jnp.einsum('bhck,bhcv->bhkv', k_c * jnp.exp(g_last - gc_c)[..., None], v_new))
        return state, o_c

    state0 = jnp.zeros((bsz, H, Dh, Dh), jnp.float32)
    _, o = lax.scan(chunk_step, state0, xs)
    o = jnp.transpose(o, (1, 0, 3, 2, 4)).reshape(bsz, seq, H, Dh)
    zf = z.astype(jnp.float32).reshape(bsz, seq, H, Dh)
    o = rms_norm(o, norm_w) * jax.nn.silu(zf)
    return o.reshape(bsz, seq, GDN_WIDTH)


def stick_breaking_attention(q, k, v):
    bsz, seq, _ = q.shape
    H, Dh = SB_HEADS, SB_HEAD_DIM

    def heads(t):
        return t.astype(jnp.float32).reshape(bsz, seq, H, Dh).transpose(0, 2, 1, 3)

    qh = heads(q) * (Dh ** -0.5)
    kh = heads(k)
    vh = heads(v)
    outs = []
    for blk in range(seq // SB_BLOCK):
        start, end = blk * SB_BLOCK, (blk + 1) * SB_BLOCK
        logits = jnp.einsum('bhtd,bhsd->bhts', qh[:, :, start:end], kh[:, :, :end])
        causal = jnp.arange(end)[None, :] < (start + jnp.arange(SB_BLOCK))[:, None]
        log_remain = jnp.where(causal, jax.nn.log_sigmoid(-logits), 0.0)
        log_after = lax.cumsum(log_remain, axis=3, reverse=True) - log_remain
        weights = jnp.where(causal, jnp.exp(jax.nn.log_sigmoid(logits) + log_after), 0.0)
        outs.append(jnp.einsum('bhts,bhsd->bhtd', weights, vh[:, :, :end]))
    o = jnp.concatenate(outs, axis=2)
    return o.transpose(0, 2, 1, 3)


def hierarchical_moe(h, w_rg, b_rg, w_re, b_re, w_gate, w_up, w_down):
    bsz, seq, d = h.shape
    n_tok = bsz * seq
    ht = h.reshape(n_tok, d)
    hf = ht.astype(jnp.float32)
    group_prob = jax.nn.softmax(hf @ w_rg.astype(jnp.float32) + b_rg.astype(jnp.float32), axis=-1)
    group_w, group_idx = lax.top_k(group_prob, 1)
    expert_logits = (hf @ w_re.astype(jnp.float32) + b_re.astype(jnp.float32)).reshape(
        n_tok, N_GROUPS, EXPERTS_PER_GROUP)
    in_group = jnp.take_along_axis(expert_logits, group_idx[:, :, None], axis=1)[:, 0]
    expert_prob = jax.nn.softmax(in_group, axis=-1)
    expert_w, expert_idx = lax.top_k(expert_prob, TOP_K)
    expert_w = expert_w / jnp.sum(expert_w, axis=-1, keepdims=True)
    gate = group_w * expert_w
    expert_id = group_idx * EXPERTS_PER_GROUP + expert_idx

    n_assign = n_tok * TOP_K
    eid = expert_id.reshape(n_assign)
    tok = jnp.repeat(jnp.arange(n_tok, dtype=jnp.int32), TOP_K)
    wts = gate.reshape(n_assign)
    order = jnp.argsort(eid)
    eid_s, tok_s, wts_s = eid[order], tok[order], wts[order]
    counts = jnp.bincount(eid, length=N_EXPERTS)
    starts = jnp.cumsum(counts) - counts
    padded = (counts + MOE_BLOCK - 1) // MOE_BLOCK * MOE_BLOCK
    padded_end = jnp.cumsum(padded)
    padded_start = padded_end - padded
    dest = padded_start[eid_s] + (jnp.arange(n_assign) - starts[eid_s])
    n_blocks = (n_assign + MOE_BLOCK - 1) // MOE_BLOCK + N_EXPERTS
    n_rows = n_blocks * MOE_BLOCK
    buf = jnp.zeros((n_rows, d), ht.dtype).at[dest].set(ht[tok_s])
    block_expert = jnp.minimum(
        jnp.searchsorted(padded_end, jnp.arange(n_blocks) * MOE_BLOCK, side='right'),
        N_EXPERTS - 1)

    def expert_block(args):
        xb, e = args
        hidden = jax.nn.silu(xb @ w_gate[e]) * (xb @ w_up[e])
        return hidden @ w_down[e]

    y_buf = lax.map(expert_block, (buf.reshape(n_blocks, MOE_BLOCK, d), block_expert))
    y_buf = y_buf.reshape(n_rows, d)
    out = jnp.zeros((n_tok, d), jnp.float32).at[tok_s].add(
        y_buf[dest].astype(jnp.float32) * wts_s[:, None])
    return out.reshape(bsz, seq, d).astype(h.dtype)


def setup_inputs(seed: int = 0) -> dict:
    key = jax.random.key(seed)
    ks = jax.random.split(key, 22)
    f32 = jnp.float32

    def normal(k, shape, scale):
        return jax.random.normal(k, shape, f32) * scale

    def gain(k, shape):
        return 1.0 + 0.02 * jax.random.normal(k, shape, f32)

    return {
        'x': normal(ks[0], (BATCH, SEQ, D_MODEL), 1.0),
        'p': normal(ks[1], (DEPTH, BATCH, SEQ, PLE_DIM), 1.0),
        'ln_mix': gain(ks[2], (DEPTH, D_MODEL)),
        'w_in': normal(ks[3], (DEPTH, D_MODEL, IN_COLS), D_MODEL ** -0.5),
        'conv_w': normal(ks[4], (DEPTH, GDN_CONV, 3 * GDN_WIDTH), GDN_CONV ** -0.5),
        'a_log': jnp.log(jax.random.uniform(ks[5], (DEPTH, GDN_HEADS), f32, 1.0, 16.0)),
        'dt_bias': normal(ks[6], (DEPTH, GDN_HEADS), 0.1),
        'gdn_norm_w': gain(ks[7], (DEPTH, GDN_HEAD_DIM)),
        'sb_norm_w': gain(ks[8], (DEPTH, SB_HEAD_DIM)),
        'w_out': normal(ks[9], (DEPTH, D_MIX, D_MODEL), D_MIX ** -0.5),
        'ln_moe': gain(ks[10], (DEPTH, D_MODEL)),
        'w_router_group': normal(ks[11], (DEPTH, D_MODEL, N_GROUPS), D_MODEL ** -0.5),
        'b_router_group': normal(ks[12], (DEPTH, N_GROUPS), 0.01),
        'w_router_expert': normal(ks[13], (DEPTH, D_MODEL, N_EXPERTS), D_MODEL ** -0.5),
        'b_router_expert': normal(ks[14], (DEPTH, N_EXPERTS), 0.01),
        'w_expert_gate': normal(ks[15], (DEPTH, N_EXPERTS, D_MODEL, D_EXPERT), D_MODEL ** -0.5),
        'w_expert_up': normal(ks[16], (DEPTH, N_EXPERTS, D_MODEL, D_EXPERT), D_MODEL ** -0.5),
        'w_expert_down': normal(ks[17], (DEPTH, N_EXPERTS, D_EXPERT, D_MODEL), D_EXPERT ** -0.5),
        'ln_ple': gain(ks[18], (DEPTH, D_MODEL)),
        'w_ple_gate': normal(ks[19], (DEPTH, D_MODEL, D_MODEL), D_MODEL ** -0.5),
        'w_ple_proj': normal(ks[20], (DEPTH, PLE_DIM, D_MODEL), PLE_DIM ** -0.5),
        'ln_final': gain(ks[21], (D_MODEL,)),
    }


def reference(x, p, ln_mix, w_in, conv_w, a_log, dt_bias, gdn_norm_w, sb_norm_w, w_out,
              ln_moe, w_router_group, b_router_group, w_router_expert, b_router_expert,
              w_expert_gate, w_expert_up, w_expert_down, ln_ple, w_ple_gate, w_ple_proj,
              ln_final):
    bsz, seq, _ = x.shape
    splits = np.cumsum([3 * GDN_WIDTH, GDN_WIDTH, GDN_HEADS, GDN_HEADS, SB_WIDTH, SB_WIDTH]).tolist()
    for i in range(DEPTH):
        h = rms_norm(x, ln_mix[i])
        proj = jnp.einsum('bsd,dc->bsc', h, w_in[i])
        gdn_qkv, gdn_z, gdn_b, gdn_a, sb_q, sb_k, sb_v = jnp.split(proj, splits, axis=-1)
        gdn_out = gated_deltanet(gdn_qkv, gdn_z, gdn_b, gdn_a, conv_w[i], a_log[i],
                                 dt_bias[i], gdn_norm_w[i])
        sb_out = rms_norm(stick_breaking_attention(sb_q, sb_k, sb_v), sb_norm_w[i])
        sb_out = sb_out.reshape(bsz, seq, SB_WIDTH)
        mixed = jnp.concatenate([gdn_out.astype(x.dtype), sb_out.astype(x.dtype)], axis=-1)
        x = x + jnp.einsum('bsc,cd->bsd', mixed, w_out[i])
        x = x + hierarchical_moe(rms_norm(x, ln_moe[i]), w_router_group[i], b_router_group[i],
                                 w_router_expert[i], b_router_expert[i], w_expert_gate[i],
                                 w_expert_up[i], w_expert_down[i])
        hp = rms_norm(x, ln_ple[i])
        ple_gate = jax.nn.sigmoid(jnp.einsum('bsd,de->bse', hp, w_ple_gate[i]).astype(jnp.float32))
        ple = jnp.einsum('bsp,pd->bsd', p[i], w_ple_proj[i]).astype(jnp.float32)
        x = x + (ple_gate * ple).astype(x.dtype)
    return rms_norm(x, ln_final)
```

```python
import functools

import jax
import jax.numpy as jnp
from jax import lax
from jax.experimental import pallas as pl
from jax.experimental.pallas import tpu as pltpu

F32 = jnp.float32
BF16 = jnp.bfloat16
I32 = jnp.int32

D_MODEL = 1024
PLE_DIM = 256
GDN_HEADS = 4
GDN_HEAD_DIM = 128
GDN_WIDTH = GDN_HEADS * GDN_HEAD_DIM
GDN_CONV = 4
GDN_CHUNK = 64
SB_HEADS = 8
SB_HEAD_DIM = 64
SB_WIDTH = SB_HEADS * SB_HEAD_DIM
SB_BLOCK = 128
N_GROUPS = 8
EXPERTS_PER_GROUP = 8
N_EXPERTS = N_GROUPS * EXPERTS_PER_GROUP
D_EXPERT = D_MODEL // 4
MOE_BLOCK = 128
EPS = 1e-6

LANES = 128
SUBLANES = 8
ROW_TILES = D_MODEL // LANES
PROJ_W = 3 * GDN_WIDTH + GDN_WIDTH + 3 * SB_WIDTH
PROJ_CHUNK = 512
PREV_ROWS = 16
VMEM_LIMIT = 48 * 1024 * 1024

IN_TM = 512
POST_TM = 256
DISP_TM = 256
FINAL_TM = 256


def _cparams(n_axes):
    return pltpu.CompilerParams(dimension_semantics=("arbitrary",) * n_axes,
                                vmem_limit_bytes=VMEM_LIMIT)


def _rms(x, w):
    return x * lax.rsqrt(jnp.mean(x * x, axis=-1, keepdims=True) + EPS) * w


def _mm(a, b):
    return jnp.dot(a.astype(BF16), b.astype(BF16), preferred_element_type=F32)


def _mm_nt(a, b):
    return lax.dot_general(a.astype(BF16), b.astype(BF16), (((1,), (1,)), ((), ())),
                           preferred_element_type=F32)


def _mm_tn(a, b):
    return lax.dot_general(a.astype(BF16), b.astype(BF16), (((0,), (0,)), ((), ())),
                           preferred_element_type=F32)


def _split(a):
    hi = a.astype(BF16)
    lo = (a - hi.astype(F32)).astype(BF16)
    return hi, lo


def _softplus(x):
    return jnp.maximum(x, 0.0) + jnp.log(1.0 + jnp.exp(-jnp.abs(x)))


def _in_proj_kernel(x_ref, ln_ref, w_ref, proj_ref, ba_ref):
    h = _rms(x_ref[...], ln_ref[...]).astype(BF16)
    for j in range(0, PROJ_W, PROJ_CHUNK):
        proj_ref[:, j:j + PROJ_CHUNK] = jnp.dot(
            h, w_ref[:, j:j + PROJ_CHUNK], preferred_element_type=F32).astype(BF16)
    ba_ref[...] = jnp.dot(h, w_ref[:, PROJ_W:PROJ_W + LANES], preferred_element_type=F32)


def _in_proj(x2d, ln, w_all):
    n_tok = x2d.shape[0]
    tm = min(IN_TM, n_tok)
    return pl.pallas_call(
        _in_proj_kernel,
        grid=(n_tok // tm,),
        in_specs=[pl.BlockSpec((tm, D_MODEL), lambda i: (i, 0)),
                  pl.BlockSpec((1, D_MODEL), lambda i: (0, 0)),
                  pl.BlockSpec((D_MODEL, PROJ_W + LANES), lambda i: (0, 0))],
        out_specs=[pl.BlockSpec((tm, PROJ_W), lambda i: (i, 0)),
                   pl.BlockSpec((tm, LANES), lambda i: (i, 0))],
        out_shape=[jax.ShapeDtypeStruct((n_tok, PROJ_W), BF16),
                   jax.ShapeDtypeStruct((n_tok, LANES), F32)],
        compiler_params=_cparams(1),
        name="in_proj",
    )(x2d, ln, w_all)


def _gdn_kernel(cur_ref, prev_ref, z_ref, ba_ref, convw_ref, gp_ref, nw_ref, out_ref, state_ref):
    C, H, Dh = GDN_CHUNK, GDN_HEADS, GDN_HEAD_DIM
    c = pl.program_id(1)

    @pl.when(c == 0)
    def _():
        state_ref[...] = jnp.zeros_like(state_ref)

    cur = cur_ref[0].astype(F32)
    prev = jnp.where(c > 0, prev_ref[0].astype(F32), 0.0)
    stacked = jnp.concatenate([prev, cur], axis=0)
    w = convw_ref[...]
    y = w[GDN_CONV - 1:GDN_CONV, :] * cur
    for k in range(GDN_CONV - 1):
        shift = GDN_CONV - 1 - k
        y = y + w[k:k + 1, :] * pltpu.roll(stacked, shift, axis=0)[PREV_ROWS:, :]
    qkv = y * jax.nn.sigmoid(y)

    ri = lax.broadcasted_iota(I32, (C, C), 0)
    ci = lax.broadcasted_iota(I32, (C, C), 1)
    incl = ci <= ri
    strict = ci < ri
    diag = ci == ri
    eye = jnp.where(diag, 1.0, 0.0).astype(F32)

    ba = ba_ref[0, 0]
    gp = gp_ref[...]
    beta_rows = jax.nn.sigmoid(ba[0:H, :])
    g_rows = -jnp.exp(gp[0:H, :]) * _softplus(ba[H:2 * H, :] + gp[H:2 * H, :])

    for h in range(H):
        q = qkv[:, h * Dh:(h + 1) * Dh]
        k = qkv[:, GDN_WIDTH + h * Dh:GDN_WIDTH + (h + 1) * Dh]
        v = qkv[:, 2 * GDN_WIDTH + h * Dh:2 * GDN_WIDTH + (h + 1) * Dh]
        q = q * lax.rsqrt(jnp.sum(q * q, axis=-1, keepdims=True) + EPS) * (Dh ** -0.5)
        k = k * lax.rsqrt(jnp.sum(k * k, axis=-1, keepdims=True) + EPS)

        g_b = jnp.broadcast_to(g_rows[h:h + 1, :], (C, C))
        beta_b = jnp.broadcast_to(beta_rows[h:h + 1, :], (C, C))
        gc_col = jnp.sum(jnp.where(incl, g_b, 0.0), axis=1, keepdims=True)
        beta_col = jnp.sum(jnp.where(diag, beta_b, 0.0), axis=1, keepdims=True)
        gc_row = jnp.sum(jnp.where(diag, jnp.broadcast_to(gc_col, (C, C)), 0.0), axis=0, keepdims=True)
        decay = jnp.where(incl, jnp.exp(jnp.where(incl, gc_col - gc_row, 0.0)), 0.0)

        neg_l = jnp.where(strict, -(beta_col * _mm_nt(k, k) * decay), 0.0)
        t_inv = eye + neg_l
        p = neg_l
        for _ in range(5):
            p = _mm(p, p)
            t_inv = t_inv + _mm(t_inv, p)

        exp_gc = jnp.exp(gc_col)
        rhs = jnp.concatenate([v * beta_col, k * (beta_col * exp_gc)], axis=1)
        sol = _mm(t_inv, rhs)
        u, wk = sol[:, :Dh], sol[:, Dh:]
        qk = _mm_nt(q, k) * decay

        s = state_ref[h]
        v_new = u - _mm(wk, s)
        o = _mm(q * exp_gc, s) + _mm(qk, v_new)
        g_last = gc_col[C - 1:C, :]
        state_ref[h] = s * jnp.exp(g_last) + _mm_tn(k * jnp.exp(g_last - gc_col), v_new)

        zz = z_ref[0][:, h * Dh:(h + 1) * Dh].astype(F32)
        out_ref[0, :, h * Dh:(h + 1) * Dh] = (_rms(o, nw_ref[...]) * (zz * jax.nn.sigmoid(zz))).astype(BF16)


def _gdn(proj3, ba_rows, conv_w, gparams, norm_w):
    bsz, seq, _ = proj3.shape
    C = GDN_CHUNK
    n_chunks = seq // C
    prev_per_chunk = C // PREV_ROWS
    return pl.pallas_call(
        _gdn_kernel,
        grid=(bsz, n_chunks),
        in_specs=[
            pl.BlockSpec((1, C, 3 * GDN_WIDTH), lambda b, c: (b, c, 0)),
            pl.BlockSpec((1, PREV_ROWS, 3 * GDN_WIDTH),
                         lambda b, c: (b, jnp.maximum(c * prev_per_chunk - 1, 0), 0)),
            pl.BlockSpec((1, C, GDN_WIDTH), lambda b, c: (b, c, 3)),
            pl.BlockSpec((1, 1, 2 * GDN_HEADS, C), lambda b, c: (b, c, 0, 0)),
            pl.BlockSpec((GDN_CONV, 3 * GDN_WIDTH), lambda b, c: (0, 0)),
            pl.BlockSpec((2 * GDN_HEADS, C), lambda b, c: (0, 0)),
            pl.BlockSpec((1, GDN_HEAD_DIM), lambda b, c: (0, 0)),
        ],
        out_specs=pl.BlockSpec((1, C, GDN_WIDTH), lambda b, c: (b, c, 0)),
        out_shape=jax.ShapeDtypeStruct((bsz, seq, GDN_WIDTH), BF16),
        scratch_shapes=[pltpu.VMEM((GDN_HEADS, GDN_HEAD_DIM, GDN_HEAD_DIM), F32)],
        compiler_params=_cparams(2),
        name="gdn",
    )(proj3, proj3, proj3, ba_rows, conv_w, gparams, norm_w)


def _sb_kernel(q_ref, k_ref, v_ref, nw_ref, out_ref):
    blk, Dh = SB_BLOCK, SB_HEAD_DIM
    qb = pl.program_id(2)
    row = lax.broadcasted_iota(I32, (blk, blk), 0)
    lane = lax.broadcasted_iota(I32, (blk, blk), 1)
    low = lane < Dh
    causal = lane < row

    q = q_ref[0]
    zeros = jnp.zeros_like(q)
    q_heads = (jnp.where(low, q, zeros), jnp.where(low, zeros, q))
    suffix_ones = jnp.where(row >= lane, 1.0, 0.0).astype(BF16)
    suffix2 = jnp.concatenate([suffix_ones, suffix_ones], axis=0)

    def step(kb, carry, on_diagonal):
        start = pl.multiple_of(kb * blk, blk)
        k_blk = k_ref[0, pl.ds(start, blk), :]
        v_blk = v_ref[0, pl.ds(start, blk), :]
        out = []
        for i in range(2):
            later, acc = carry[i]
            z = lax.dot_general(q_heads[i], k_blk, (((1,), (1,)), ((), ())),
                                preferred_element_type=F32) * (Dh ** -0.5)
            log_remain = -_softplus(z)
            if on_diagonal:
                log_remain = jnp.where(causal, log_remain, 0.0)
            hi, lo = _split(log_remain)
            incl_after = jnp.dot(jnp.concatenate([hi, lo], axis=1), suffix2,
                                 preferred_element_type=F32)
            weights = jnp.exp(z + incl_after + later)
            if on_diagonal:
                weights = jnp.where(causal, weights, 0.0)
            acc = acc + jnp.dot(weights.astype(BF16), v_blk, preferred_element_type=F32)
            out.append((later + incl_after[:, 0:1], acc))
        return tuple(out)

    init = tuple((jnp.zeros((blk, 1), F32), jnp.zeros((blk, blk), F32)) for _ in range(2))
    carry = step(qb, init, True)
    carry = lax.fori_loop(0, qb, lambda i, cr: step(qb - 1 - i, cr, False), carry)

    o = jnp.where(low, carry[0][1], carry[1][1])
    same_head = (row // Dh) == (lane // Dh)
    avg = jnp.where(same_head, 1.0 / Dh, 0.0).astype(BF16)
    hi, lo = _split(o * o)
    ms = jnp.dot(jnp.concatenate([hi, lo], axis=1), jnp.concatenate([avg, avg], axis=0),
                 preferred_element_type=F32)
    out_ref[0] = (o * lax.rsqrt(ms + EPS) * nw_ref[...]).astype(BF16)


def _sb(proj3, norm_w2):
    bsz, seq, _ = proj3.shape
    pairs = SB_WIDTH // LANES
    q_col = (3 * GDN_WIDTH + GDN_WIDTH) // LANES
    k_col = q_col + pairs
    v_col = k_col + pairs
    return pl.pallas_call(
        _sb_kernel,
        grid=(bsz, pairs, seq // SB_BLOCK),
        in_specs=[
            pl.BlockSpec((1, SB_BLOCK, LANES), lambda b, hp, qb: (b, qb, q_col + hp)),
            pl.BlockSpec((1, seq, LANES), lambda b, hp, qb: (b, 0, k_col + hp)),
            pl.BlockSpec((1, seq, LANES), lambda b, hp, qb: (b, 0, v_col + hp)),
            pl.BlockSpec((1, LANES), lambda b, hp, qb: (0, 0)),
        ],
        out_specs=pl.BlockSpec((1, SB_BLOCK, LANES), lambda b, hp, qb: (b, qb, hp)),
        out_shape=jax.ShapeDtypeStruct((bsz, seq, SB_WIDTH), BF16),
        compiler_params=_cparams(3),
        name="sb_attn",
    )(proj3, proj3, proj3, norm_w2)


def _post_mix_kernel(x_ref, g_ref, s_ref, wo_ref, ln_ref, wrt_ref, br_ref,
                     x1_ref, h2r_ref, route_ref, gcol_ref, cnt_ref, count_ref):
    tm = x_ref.shape[0]
    i = pl.program_id(0)

    @pl.when(i == 0)
    def _():
        count_ref[...] = jnp.zeros_like(count_ref)

    x1 = (x_ref[...]
          + jnp.dot(g_ref[...], wo_ref[0:GDN_WIDTH, :], preferred_element_type=F32)
          + jnp.dot(s_ref[...], wo_ref[GDN_WIDTH:, :], preferred_element_type=F32))
    x1_ref[...] = x1
    h2 = _rms(x1, ln_ref[...])
    for j in range(ROW_TILES):
        h2r_ref[:, j, :] = h2[:, j * LANES:(j + 1) * LANES]

    w_hi, w_lo = _split(wrt_ref[...])
    h_hi, h_lo = _split(h2)
    nt = (((1,), (1,)), ((), ()))
    logits = (lax.dot_general(w_hi, h_hi, nt, preferred_element_type=F32)
              + lax.dot_general(w_hi, h_lo, nt, preferred_element_type=F32)
              + lax.dot_general(w_lo, h_hi, nt, preferred_element_type=F32)) + br_ref[:, 0:1]

    g_logits = logits[N_EXPERTS:N_EXPERTS + N_GROUPS, :]
    g_max = jnp.max(g_logits, axis=0, keepdims=True)
    group_w = 1.0 / jnp.sum(jnp.exp(g_logits - g_max), axis=0, keepdims=True)
    sub8 = lax.broadcasted_iota(I32, (N_GROUPS, tm), 0)
    g_idx = jnp.min(jnp.where(g_logits == g_max, sub8, N_GROUPS), axis=0, keepdims=True)

    e_logits = logits[0:N_EXPERTS, :]
    erow = lax.broadcasted_iota(I32, (N_EXPERTS, tm), 0)
    neg_inf = -jnp.inf
    in_group = jnp.where((erow // EXPERTS_PER_GROUP) == g_idx, e_logits, neg_inf)
    m1 = jnp.max(in_group, axis=0, keepdims=True)
    idx1 = jnp.min(jnp.where(in_group == m1, erow, N_EXPERTS), axis=0, keepdims=True)
    rest = jnp.where(erow == idx1, neg_inf, in_group)
    m2 = jnp.max(rest, axis=0, keepdims=True)
    idx2 = jnp.min(jnp.where(rest == m2, erow, N_EXPERTS), axis=0, keepdims=True)
    r = jnp.exp(m2 - m1)
    w1 = 1.0 / (1.0 + r)
    gate1 = group_w * w1
    gate2 = group_w * (r * w1)

    oh1 = erow == idx1
    oh2 = erow == idx2
    picked = jnp.where(oh1 | oh2, 1.0, 0.0)
    tr = lax.broadcasted_iota(I32, (tm, tm), 0)
    tc = lax.broadcasted_iota(I32, (tm, tm), 1)
    before = jnp.where(tr < tc, 1.0, 0.0).astype(BF16)
    seen = jnp.dot(picked.astype(BF16), before, preferred_element_type=F32) + count_ref[:, 0:1]
    rank1 = jnp.sum(jnp.where(oh1, seen, 0.0), axis=0, keepdims=True)
    rank2 = jnp.sum(jnp.where(oh2, seen, 0.0), axis=0, keepdims=True)
    count_ref[...] = count_ref[...] + jnp.sum(picked, axis=1, keepdims=True)
    cnt_ref[...] = count_ref[...]

    sub = lax.broadcasted_iota(I32, (SUBLANES, tm), 0)
    rows = (idx1.astype(F32), idx2.astype(F32), rank1, rank2)
    route = jnp.zeros((SUBLANES, tm), F32)
    for n, val in enumerate(rows):
        route = jnp.where(sub == n, val, route)
    route_ref[...] = route

    sub_l = lax.broadcasted_iota(I32, (LANES, tm), 0)
    gates_t = jnp.where(sub_l == 0, gate1, jnp.where(sub_l == 1, gate2, 0.0))
    gcol_ref[...] = gates_t.T


def _post_mix(x2d, gdn_out, sb_out, w_out, ln, w_router_t, b_router):
    n_tok = x2d.shape[0]
    tm = min(POST_TM, n_tok)
    return pl.pallas_call(
        _post_mix_kernel,
        grid=(n_tok // tm,),
        in_specs=[
            pl.BlockSpec((tm, D_MODEL), lambda i: (i, 0)),
            pl.BlockSpec((tm, GDN_WIDTH), lambda i: (i, 0)),
            pl.BlockSpec((tm, SB_WIDTH), lambda i: (i, 0)),
            pl.BlockSpec((D_MODEL, D_MODEL), lambda i: (0, 0)),
            pl.BlockSpec((1, D_MODEL), lambda i: (0, 0)),
            pl.BlockSpec((LANES, D_MODEL), lambda i: (0, 0)),
            pl.BlockSpec((LANES, LANES), lambda i: (0, 0)),
        ],
        out_specs=[
            pl.BlockSpec((tm, D_MODEL), lambda i: (i, 0)),
            pl.BlockSpec((tm, ROW_TILES, LANES), lambda i: (i, 0, 0)),
            pl.BlockSpec((SUBLANES, tm), lambda i: (0, i)),
            pl.BlockSpec((tm, LANES), lambda i: (i, 0)),
            pl.BlockSpec((N_EXPERTS, LANES), lambda i: (0, 0)),
        ],
        out_shape=[
            jax.ShapeDtypeStruct((n_tok, D_MODEL), F32),
            jax.ShapeDtypeStruct((n_tok, ROW_TILES, LANES), F32),
            jax.ShapeDtypeStruct((SUBLANES, n_tok), F32),
            jax.ShapeDtypeStruct((n_tok, LANES), F32),
            jax.ShapeDtypeStruct((N_EXPERTS, LANES), F32),
        ],
        scratch_shapes=[pltpu.VMEM((N_EXPERTS, LANES), F32)],
        compiler_params=_cparams(1),
        name="post_mix",
    )(x2d, gdn_out, sb_out, w_out, ln, w_router_t, b_router)


def _row_copy(src_hbm, src_row, dst_hbm, dst_row, sem):
    return pltpu.make_async_copy(src_hbm.at[src_row], dst_hbm.at[dst_row], sem)


def _dispatch_kernel(d1_ref, d2_ref, h2r_hbm, buf_in_hbm, buf_hbm, sem):
    del buf_in_hbm
    tm = d1_ref.shape[2]
    base = pl.program_id(0) * tm

    def issue(t, carry):
        _row_copy(h2r_hbm, base + t, buf_hbm, d1_ref[0, 0, t], sem).start()
        _row_copy(h2r_hbm, base + t, buf_hbm, d2_ref[0, 0, t], sem).start()
        return carry

    def drain(t, carry):
        _row_copy(h2r_hbm, base + t, buf_hbm, d1_ref[0, 0, t], sem).wait()
        _row_copy(h2r_hbm, base + t, buf_hbm, d2_ref[0, 0, t], sem).wait()
        return carry

    lax.fori_loop(0, tm, issue, 0)
    lax.fori_loop(0, tm, drain, 0)


def _dispatch(dest1, dest2, h2r, n_rows):
    n_tok = h2r.shape[0]
    tm = min(DISP_TM, n_tok)
    n_steps = n_tok // tm
    d1 = dest1.reshape(n_steps, 1, tm)
    d2 = dest2.reshape(n_steps, 1, tm)
    smem_spec = pl.BlockSpec((1, 1, tm), lambda i: (i, 0, 0), memory_space=pltpu.SMEM)
    buf0 = jnp.zeros((n_rows, ROW_TILES, LANES), F32)
    return pl.pallas_call(
        _dispatch_kernel,
        grid=(n_steps,),
        in_specs=[smem_spec, smem_spec,
                  pl.BlockSpec(memory_space=pl.ANY),
                  pl.BlockSpec(memory_space=pl.ANY)],
        out_specs=pl.BlockSpec(memory_space=pl.ANY),
        out_shape=jax.ShapeDtypeStruct((n_rows, ROW_TILES, LANES), F32),
        scratch_shapes=[pltpu.SemaphoreType.DMA(())],
        input_output_aliases={3: 0},
        compiler_params=_cparams(1),
        name="moe_dispatch",
    )(d1, d2, h2r, buf0)


def _ffn_kernel(be_ref, x_ref, wgu_ref, wd_ref, y_ref):
    del be_ref
    acc = jnp.zeros((MOE_BLOCK, 2 * D_EXPERT), F32)
    for j in range(ROW_TILES):
        acc = acc + jnp.dot(x_ref[:, j, :].astype(BF16), wgu_ref[0, j * LANES:(j + 1) * LANES, :],
                            preferred_element_type=F32)
    gate, up = acc[:, :D_EXPERT], acc[:, D_EXPERT:]
    hidden = (gate * jax.nn.sigmoid(gate)) * up
    y = jnp.dot(hidden.astype(BF16), wd_ref[0], preferred_element_type=F32)
    for j in range(ROW_TILES):
        y_ref[:, j, :] = y[:, j * LANES:(j + 1) * LANES]


def _ffn(block_expert, buf, w_gate_up, w_down):
    n_rows = buf.shape[0]
    n_blocks = n_rows // MOE_BLOCK
    grid_spec = pltpu.PrefetchScalarGridSpec(
        num_scalar_prefetch=1,
        grid=(n_blocks,),
        in_specs=[
            pl.BlockSpec((MOE_BLOCK, ROW_TILES, LANES), lambda i, be: (i, 0, 0)),
            pl.BlockSpec((1, D_MODEL, 2 * D_EXPERT), lambda i, be: (be[i], 0, 0)),
            pl.BlockSpec((1, D_EXPERT, D_MODEL), lambda i, be: (be[i], 0, 0)),
        ],
        out_specs=pl.BlockSpec((MOE_BLOCK, ROW_TILES, LANES), lambda i, be: (i, 0, 0)),
    )
    return pl.pallas_call(
        _ffn_kernel,
        grid_spec=grid_spec,
        out_shape=jax.ShapeDtypeStruct((n_rows, ROW_TILES, LANES), F32),
        compiler_params=_cparams(1),
        name="moe_ffn",
    )(block_expert, buf, w_gate_up, w_down)


def _final_kernel(d1_ref, d2_ref, ybuf_hbm, x1_ref, gcol_ref, p_ref, wpg_ref, wpp_ref,
                  lnp_ref, lnf_ref, out_ref, y1_ref, y2_ref, sem):
    tm = x1_ref.shape[0]

    def issue(t, carry):
        _row_copy(ybuf_hbm, d1_ref[0, 0, t], y1_ref, t, sem).start()
        _row_copy(ybuf_hbm, d2_ref[0, 0, t], y2_ref, t, sem).start()
        return carry

    def drain(t, carry):
        _row_copy(ybuf_hbm, d1_ref[0, 0, t], y1_ref, t, sem).wait()
        _row_copy(ybuf_hbm, d2_ref[0, 0, t], y2_ref, t, sem).wait()
        return carry

    lax.fori_loop(0, tm, issue, 0)
    lax.fori_loop(0, tm, drain, 0)

    g1 = gcol_ref[:, 0:1]
    g2 = gcol_ref[:, 1:2]
    moe = jnp.concatenate([g1 * y1_ref[:, j, :] + g2 * y2_ref[:, j, :] for j in range(ROW_TILES)], axis=1)
    x2 = x1_ref[...] + moe
    hp = _rms(x2, lnp_ref[...]).astype(BF16)
    ple_gate = jax.nn.sigmoid(jnp.dot(hp, wpg_ref[...], preferred_element_type=F32))
    ple = jnp.dot(p_ref[...].astype(BF16), wpp_ref[...], preferred_element_type=F32)
    x3 = x2 + ple_gate * ple
    out_ref[...] = _rms(x3, lnf_ref[...])


def _final(dest1, dest2, ybuf, x1, gcol, p2d, w_ple_gate, w_ple_proj, ln_ple, ln_final):
    n_tok = x1.shape[0]
    tm = min(FINAL_TM, n_tok)
    n_steps = n_tok // tm
    d1 = dest1.reshape(n_steps, 1, tm)
    d2 = dest2.reshape(n_steps, 1, tm)
    smem_spec = pl.BlockSpec((1, 1, tm), lambda i: (i, 0, 0), memory_space=pltpu.SMEM)
    return pl.pallas_call(
        _final_kernel,
        grid=(n_steps,),
        in_specs=[
            smem_spec, smem_spec,
            pl.BlockSpec(memory_space=pl.ANY),
            pl.BlockSpec((tm, D_MODEL), lambda i: (i, 0)),
            pl.BlockSpec((tm, LANES), lambda i: (i, 0)),
            pl.BlockSpec((tm, PLE_DIM), lambda i: (i, 0)),
            pl.BlockSpec((D_MODEL, D_MODEL), lambda i: (0, 0)),
            pl.BlockSpec((PLE_DIM, D_MODEL), lambda i: (0, 0)),
            pl.BlockSpec((1, D_MODEL), lambda i: (0, 0)),
            pl.BlockSpec((1, D_MODEL), lambda i: (0, 0)),
        ],
        out_specs=pl.BlockSpec((tm, D_MODEL), lambda i: (i, 0)),
        out_shape=jax.ShapeDtypeStruct((n_tok, D_MODEL), F32),
        scratch_shapes=[pltpu.VMEM((tm, ROW_TILES, LANES), F32),
                        pltpu.VMEM((tm, ROW_TILES, LANES), F32),
                        pltpu.SemaphoreType.DMA(())],
        compiler_params=_cparams(1),
        name="moe_combine_ple_final",
    )(d1, d2, ybuf, x1, gcol, p2d, w_ple_gate, w_ple_proj, ln_ple, ln_final)


def _layer(x2d, p2d, bsz, seq, ln_mix, w_in, conv_w, a_log, dt_bias, gdn_norm_w, sb_norm_w, w_out,
           ln_moe, w_rg, b_rg, w_re, b_re, w_eg, w_eu, w_ed, ln_ple, w_ple_gate, w_ple_proj, ln_final):
    n_tok = bsz * seq
    C = GDN_CHUNK
    o_z = 3 * GDN_WIDTH
    o_b = o_z + GDN_WIDTH
    o_q = o_b + 2 * GDN_HEADS
    w_all = jnp.concatenate(
        [w_in[:, :o_b], w_in[:, o_q:], w_in[:, o_b:o_q],
         jnp.zeros((D_MODEL, LANES - 2 * GDN_HEADS), F32)], axis=1).astype(BF16)
    proj, ba = _in_proj(x2d, ln_mix.reshape(1, D_MODEL), w_all)
    proj3 = proj.reshape(bsz, seq, PROJ_W)
    ba_rows = ba[:, :2 * GDN_HEADS].reshape(bsz, seq // C, C, 2 * GDN_HEADS).transpose(0, 1, 3, 2)
    gparams = jnp.broadcast_to(jnp.concatenate([a_log, dt_bias])[:, None], (2 * GDN_HEADS, C)).astype(F32)

    gdn_out = _gdn(proj3, ba_rows, conv_w, gparams, gdn_norm_w.reshape(1, GDN_HEAD_DIM))
    sb_out = _sb(proj3, jnp.tile(sb_norm_w, LANES // SB_HEAD_DIM).reshape(1, LANES))

    w_router_t = jnp.zeros((LANES, D_MODEL), F32)
    w_router_t = w_router_t.at[:N_EXPERTS].set(w_re.T).at[N_EXPERTS:N_EXPERTS + N_GROUPS].set(w_rg.T)
    b_router = jnp.zeros((LANES,), F32).at[:N_EXPERTS].set(b_re).at[N_EXPERTS:N_EXPERTS + N_GROUPS].set(b_rg)
    b_router = jnp.broadcast_to(b_router[:, None], (LANES, LANES))
    x1, h2r, route, gcol, cnt = _post_mix(
        x2d, gdn_out.reshape(n_tok, GDN_WIDTH), sb_out.reshape(n_tok, SB_WIDTH),
        w_out.astype(BF16), ln_moe.reshape(1, D_MODEL), w_router_t, b_router)

    counts = cnt[:, 0].astype(I32)
    padded = (counts + MOE_BLOCK - 1) // MOE_BLOCK * MOE_BLOCK
    padded_end = jnp.cumsum(padded)
    padded_start = padded_end - padded
    route_i = route[:4].astype(I32)
    dest1 = padded_start[route_i[0]] + route_i[2]
    dest2 = padded_start[route_i[1]] + route_i[3]
    n_blocks = (2 * n_tok + MOE_BLOCK - 1) // MOE_BLOCK + N_EXPERTS
    block_expert = jnp.minimum(
        jnp.searchsorted(padded_end, jnp.arange(n_blocks, dtype=I32) * MOE_BLOCK, side='right'),
        N_EXPERTS - 1).astype(I32)

    buf = _dispatch(dest1, dest2, h2r, n_blocks * MOE_BLOCK)
    w_gate_up = jnp.concatenate([w_eg, w_eu], axis=2).astype(BF16)
    ybuf = _ffn(block_expert, buf, w_gate_up, w_ed.astype(BF16))
    return _final(dest1, dest2, ybuf, x1, gcol, p2d, w_ple_gate.astype(BF16), w_ple_proj.astype(BF16),
                  ln_ple.reshape(1, D_MODEL), ln_final.reshape(1, D_MODEL))


def kernel(x, p, ln_mix, w_in, conv_w, a_log, dt_bias, gdn_norm_w, sb_norm_w, w_out, ln_moe,
           w_router_group, b_router_group, w_router_expert, b_router_expert, w_expert_gate,
           w_expert_up, w_expert_down, ln_ple, w_ple_gate, w_ple_proj, ln_final):
    bsz, seq, _ = x.shape
    depth = p.shape[0]
    assert depth == 1, "the final RMSNorm is fused into the layer's last kernel"
    x2d = x.reshape(bsz * seq, D_MODEL)
    out = _layer(x2d, p[0].reshape(bsz * seq, PLE_DIM), bsz, seq, ln_mix[0], w_in[0], conv_w[0],
                 a_log[0], dt_bias[0], gdn_norm_w[0], sb_norm_w[0], w_out[0], ln_moe[0],
                 w_router_group[0], b_router_group[0], w_router_expert[0], b_router_expert[0],
                 w_expert_gate[0], w_expert_up[0], w_expert_down[0], ln_ple[0], w_ple_gate[0],
                 w_ple_proj[0], ln_final)
    return out.reshape(bsz, seq, D_MODEL)
```

```python
import jax
import jax.numpy as jnp
from jax import lax
from jax.experimental import pallas as pl
from jax.experimental.pallas import tpu as pltpu

F32 = jnp.float32
BF16 = jnp.bfloat16
I32 = jnp.int32

D_MODEL = 1024
PLE_DIM = 256
GDN_HEADS = 4
GDN_HEAD_DIM = 128
GDN_WIDTH = GDN_HEADS * GDN_HEAD_DIM
GDN_CONV = 4
GDN_CHUNK = 64
SB_HEADS = 8
SB_HEAD_DIM = 64
SB_WIDTH = SB_HEADS * SB_HEAD_DIM
N_GROUPS = 8
EXPERTS_PER_GROUP = 8
N_EXPERTS = N_GROUPS * EXPERTS_PER_GROUP
D_EXPERT = D_MODEL // 4
MOE_BLOCK = 128
EPS = 1e-6

LANES = 128
SUBLANES = 8
ROW_TILES = D_MODEL // LANES
PROJ_W = 3 * GDN_WIDTH + GDN_WIDTH + 3 * SB_WIDTH
PROJ_CHUNK = 512
PREV_ROWS = 16
VMEM_LIMIT = 48 * 1024 * 1024

IN_TM = 512
GDN_CHUNKS_PER_STEP = 2
GDN_HC = GDN_HEADS * GDN_CHUNK
SB_TILE = 256
POST_TM = 256
DISP_TM = 256
FINAL_TM = 256


def _cparams(n_axes):
    return pltpu.CompilerParams(dimension_semantics=("arbitrary",) * n_axes,
                                vmem_limit_bytes=VMEM_LIMIT)


def _rms(x, w):
    return x * lax.rsqrt(jnp.mean(x * x, axis=-1, keepdims=True) + EPS) * w


def _mm(a, b):
    return jnp.dot(a.astype(BF16), b.astype(BF16), preferred_element_type=F32)


def _mm_nt(a, b):
    return lax.dot_general(a.astype(BF16), b.astype(BF16), (((1,), (1,)), ((), ())),
                           preferred_element_type=F32)


def _mm_tn(a, b):
    return lax.dot_general(a.astype(BF16), b.astype(BF16), (((0,), (0,)), ((), ())),
                           preferred_element_type=F32)


def _split(a):
    hi = a.astype(BF16)
    lo = (a - hi.astype(F32)).astype(BF16)
    return hi, lo


def _softplus(x):
    return jnp.maximum(x, 0.0) + jnp.log(1.0 + jnp.exp(-jnp.abs(x)))


def _in_proj_kernel(x_ref, ln_ref, w_ref, proj_ref, ba_ref):
    h = _rms(x_ref[...], ln_ref[...]).astype(BF16)
    for j in range(0, PROJ_W, PROJ_CHUNK):
        proj_ref[:, j:j + PROJ_CHUNK] = jnp.dot(
            h, w_ref[:, j:j + PROJ_CHUNK], preferred_element_type=F32).astype(BF16)
    ba_ref[...] = jnp.dot(h, w_ref[:, PROJ_W:PROJ_W + LANES], preferred_element_type=F32)


def _in_proj(x2d, ln, w_all):
    n_tok = x2d.shape[0]
    tm = min(IN_TM, n_tok)
    return pl.pallas_call(
        _in_proj_kernel,
        grid=(n_tok // tm,),
        in_specs=[pl.BlockSpec((tm, D_MODEL), lambda i: (i, 0)),
                  pl.BlockSpec((1, D_MODEL), lambda i: (0, 0)),
                  pl.BlockSpec((D_MODEL, PROJ_W + LANES), lambda i: (0, 0))],
        out_specs=[pl.BlockSpec((tm, PROJ_W), lambda i: (i, 0)),
                   pl.BlockSpec((tm, LANES), lambda i: (i, 0))],
        out_shape=[jax.ShapeDtypeStruct((n_tok, PROJ_W), BF16),
                   jax.ShapeDtypeStruct((n_tok, LANES), F32)],
        compiler_params=_cparams(1),
        name="in_proj",
    )(x2d, ln, w_all)


def _gdn_kernel(cur_ref, prev_ref, z_ref, ba_ref, convw_ref, gp_ref, nw_ref, out_ref, state_ref):
    C, H, Dh, HC = GDN_CHUNK, GDN_HEADS, GDN_HEAD_DIM, GDN_HC
    rows_per_step = GDN_CHUNKS_PER_STEP * C
    step = pl.program_id(1)

    @pl.when(step == 0)
    def _():
        state_ref[...] = jnp.zeros_like(state_ref)

    cur = cur_ref[0].astype(F32)
    prev = jnp.where(step > 0, prev_ref[0].astype(F32), 0.0)
    stacked = jnp.concatenate([prev, cur], axis=0)
    w = convw_ref[...]
    y = w[GDN_CONV - 1:GDN_CONV, :] * cur
    for k in range(GDN_CONV - 1):
        shift = GDN_CONV - 1 - k
        y = y + w[k:k + 1, :] * pltpu.roll(stacked, shift, axis=0)[PREV_ROWS:, :]
    qkv = y * jax.nn.sigmoid(y)

    ri = lax.broadcasted_iota(I32, (HC, HC), 0)
    ci = lax.broadcasted_iota(I32, (HC, HC), 1)
    same_head = (ri // C) == (ci // C)
    incl = same_head & (ci <= ri)
    strict = same_head & (ci < ri)
    diag = ci == ri
    eye = jnp.where(diag, 1.0, 0.0).astype(F32)
    gp = gp_ref[...]

    def stack_heads(rows, base):
        return jnp.concatenate([rows[:, base + h * Dh:base + (h + 1) * Dh] for h in range(H)], axis=0)

    def preamble(n):
        rows = qkv[n * C:(n + 1) * C, :]
        q = stack_heads(rows, 0)
        k = stack_heads(rows, GDN_WIDTH)
        v = stack_heads(rows, 2 * GDN_WIDTH)
        q = q * lax.rsqrt(jnp.sum(q * q, axis=-1, keepdims=True) + EPS) * (Dh ** -0.5)
        k = k * lax.rsqrt(jnp.sum(k * k, axis=-1, keepdims=True) + EPS)

        ba = ba_ref[0, n]
        beta_row = jax.nn.sigmoid(ba[0:1, :])
        g_row = -jnp.exp(gp[0:1, :]) * _softplus(ba[1:2, :] + gp[1:2, :])
        gc_col = jnp.sum(jnp.where(incl, jnp.broadcast_to(g_row, (HC, HC)), 0.0), axis=1, keepdims=True)
        beta_col = jnp.sum(jnp.where(diag, jnp.broadcast_to(beta_row, (HC, HC)), 0.0), axis=1, keepdims=True)
        gc_row = jnp.sum(jnp.where(diag, jnp.broadcast_to(gc_col, (HC, HC)), 0.0), axis=0, keepdims=True)
        decay = jnp.where(incl, jnp.exp(jnp.where(incl, gc_col - gc_row, 0.0)), 0.0)

        neg_l = jnp.where(strict, -(beta_col * _mm_nt(k, k) * decay), 0.0)
        t_inv = eye + neg_l
        p = neg_l
        for _ in range(5):
            p = _mm(p, p)
            t_inv = t_inv + _mm(t_inv, p)

        exp_gc = jnp.exp(gc_col)
        sol = _mm(t_inv, jnp.concatenate([v * beta_col, k * (beta_col * exp_gc)], axis=1))
        qk = _mm_nt(q, k) * decay
        return dict(u=sol[:, :Dh], wk=sol[:, Dh:], qk=qk, qe=q * exp_gc, k=k, gc=gc_col)

    pre = [preamble(n) for n in range(GDN_CHUNKS_PER_STEP)]

    for n in range(GDN_CHUNKS_PER_STEP):
        c = pre[n]
        v_new, o_state = [], []
        for h in range(H):
            rows = slice(h * C, (h + 1) * C)
            ws = _mm(jnp.concatenate([c["wk"][rows], c["qe"][rows]], axis=0), state_ref[h])
            v_new.append(c["u"][rows] - ws[:C])
            o_state.append(ws[C:])
        v_new = jnp.concatenate(v_new, axis=0)
        o = jnp.concatenate(o_state, axis=0) + _mm(c["qk"], v_new)
        for h in range(H):
            rows = slice(h * C, (h + 1) * C)
            g_last = c["gc"][(h + 1) * C - 1:(h + 1) * C, :]
            k_dec = c["k"][rows] * jnp.exp(g_last - c["gc"][rows])
            state_ref[h] = state_ref[h] * jnp.exp(g_last) + _mm_tn(k_dec, v_new[rows])
        o_n = _rms(o, nw_ref[...])
        for h in range(H):
            zz = z_ref[0, n * C:(n + 1) * C, h * Dh:(h + 1) * Dh].astype(F32)
            out_ref[0, n * C:(n + 1) * C, h * Dh:(h + 1) * Dh] = (
                o_n[h * C:(h + 1) * C] * (zz * jax.nn.sigmoid(zz))).astype(BF16)


def _gdn(proj3, ba_rows, conv_w, gparams, norm_w):
    bsz, seq, _ = proj3.shape
    cps = GDN_CHUNKS_PER_STEP
    rows = cps * GDN_CHUNK
    prev_per_step = rows // PREV_ROWS
    return pl.pallas_call(
        _gdn_kernel,
        grid=(bsz, seq // rows),
        in_specs=[
            pl.BlockSpec((1, rows, 3 * GDN_WIDTH), lambda b, s: (b, s, 0)),
            pl.BlockSpec((1, PREV_ROWS, 3 * GDN_WIDTH),
                         lambda b, s: (b, jnp.maximum(s * prev_per_step - 1, 0), 0)),
            pl.BlockSpec((1, rows, GDN_WIDTH), lambda b, s: (b, s, 3)),
            pl.BlockSpec((1, cps, 2, GDN_HC), lambda b, s: (b, s, 0, 0)),
            pl.BlockSpec((GDN_CONV, 3 * GDN_WIDTH), lambda b, s: (0, 0)),
            pl.BlockSpec((2, GDN_HC), lambda b, s: (0, 0)),
            pl.BlockSpec((1, GDN_HEAD_DIM), lambda b, s: (0, 0)),
        ],
        out_specs=pl.BlockSpec((1, rows, GDN_WIDTH), lambda b, s: (b, s, 0)),
        out_shape=jax.ShapeDtypeStruct((bsz, seq, GDN_WIDTH), BF16),
        scratch_shapes=[pltpu.VMEM((GDN_HEADS, GDN_HEAD_DIM, GDN_HEAD_DIM), F32)],
        compiler_params=_cparams(2),
        name="gdn",
    )(proj3, proj3, proj3, ba_rows, conv_w, gparams, norm_w)


def _sb_kernel(q_ref, k_ref, v_ref, nw_ref, out_ref):
    T, Dh = SB_TILE, SB_HEAD_DIM
    qb = pl.program_id(2)
    row = lax.broadcasted_iota(I32, (T, T), 0)
    col = lax.broadcasted_iota(I32, (T, T), 1)
    causal = col < row
    suffix_ones = jnp.where(row >= col, 1.0, 0.0).astype(BF16)
    suffix2 = jnp.concatenate([suffix_ones, suffix_ones], axis=0)

    low = lax.broadcasted_iota(I32, (T, LANES), 1) < Dh
    q = q_ref[0] * jnp.asarray(Dh ** -0.5, BF16)
    zeros = jnp.zeros_like(q)
    q_heads = (jnp.where(low, q, zeros), jnp.where(low, zeros, q))

    def step(kb, carry, on_diagonal):
        start = pl.multiple_of(kb * T, T)
        k_blk = k_ref[0, pl.ds(start, T), :]
        v_blk = v_ref[0, pl.ds(start, T), :]
        out = []
        for i in range(2):
            later, acc = carry[i]
            z = lax.dot_general(q_heads[i], k_blk, (((1,), (1,)), ((), ())), preferred_element_type=F32)
            log_remain = -_softplus(z)
            if on_diagonal:
                log_remain = jnp.where(causal, log_remain, 0.0)
            hi, lo = _split(log_remain)
            incl_after = jnp.dot(jnp.concatenate([hi, lo], axis=1), suffix2, preferred_element_type=F32)
            weights = jnp.exp(z + incl_after + later)
            if on_diagonal:
                weights = jnp.where(causal, weights, 0.0)
            acc = acc + jnp.dot(weights.astype(BF16), v_blk, preferred_element_type=F32)
            out.append((later + incl_after[:, 0:1], acc))
        return tuple(out)

    init = tuple((jnp.zeros((T, 1), F32), jnp.zeros((T, LANES), F32)) for _ in range(2))
    carry = step(qb, init, True)
    carry = lax.fori_loop(0, qb, lambda i, cr: step(qb - 1 - i, cr, False), carry)

    o = jnp.where(low, carry[0][1], carry[1][1])
    hr = lax.broadcasted_iota(I32, (LANES, LANES), 0) // Dh
    hc = lax.broadcasted_iota(I32, (LANES, LANES), 1) // Dh
    avg = jnp.where(hr == hc, 1.0 / Dh, 0.0).astype(BF16)
    hi, lo = _split(o * o)
    ms = jnp.dot(jnp.concatenate([hi, lo], axis=1), jnp.concatenate([avg, avg], axis=0),
                 preferred_element_type=F32)
    out_ref[0] = (o * lax.rsqrt(ms + EPS) * nw_ref[...]).astype(BF16)


def _sb(proj3, norm_w2):
    bsz, seq, _ = proj3.shape
    tile = min(SB_TILE, seq)
    assert tile == SB_TILE and seq % tile == 0
    pairs = SB_WIDTH // LANES
    q_col = (3 * GDN_WIDTH + GDN_WIDTH) // LANES
    k_col = q_col + pairs
    v_col = k_col + pairs
    return pl.pallas_call(
        _sb_kernel,
        grid=(bsz, pairs, seq // tile),
        in_specs=[
            pl.BlockSpec((1, tile, LANES), lambda b, hp, qb: (b, qb, q_col + hp)),
            pl.BlockSpec((1, seq, LANES), lambda b, hp, qb: (b, 0, k_col + hp)),
            pl.BlockSpec((1, seq, LANES), lambda b, hp, qb: (b, 0, v_col + hp)),
            pl.BlockSpec((1, LANES), lambda b, hp, qb: (0, 0)),
        ],
        out_specs=pl.BlockSpec((1, tile, LANES), lambda b, hp, qb: (b, qb, hp)),
        out_shape=jax.ShapeDtypeStruct((bsz, seq, SB_WIDTH), BF16),
        compiler_params=_cparams(3),
        name="sb_attn",
    )(proj3, proj3, proj3, norm_w2)


def _post_mix_kernel(x_ref, g_ref, s_ref, wo_ref, ln_ref, wrt_ref, br_ref,
                     x1_ref, h2r_ref, route_ref, gcol_ref, cnt_ref, count_ref):
    tm = x_ref.shape[0]
    i = pl.program_id(0)

    @pl.when(i == 0)
    def _():
        count_ref[...] = jnp.zeros_like(count_ref)

    x1 = (x_ref[...]
          + jnp.dot(g_ref[...], wo_ref[0:GDN_WIDTH, :], preferred_element_type=F32)
          + jnp.dot(s_ref[...], wo_ref[GDN_WIDTH:, :], preferred_element_type=F32))
    x1_ref[...] = x1
    h2 = _rms(x1, ln_ref[...])
    for j in range(ROW_TILES):
        h2r_ref[:, j, :] = h2[:, j * LANES:(j + 1) * LANES]

    w_hi, w_lo = _split(wrt_ref[...])
    h_hi, h_lo = _split(h2)
    nt = (((1,), (1,)), ((), ()))
    logits = (lax.dot_general(w_hi, h_hi, nt, preferred_element_type=F32)
              + lax.dot_general(w_hi, h_lo, nt, preferred_element_type=F32)
              + lax.dot_general(w_lo, h_hi, nt, preferred_element_type=F32)) + br_ref[:, 0:1]

    g_logits = logits[N_EXPERTS:N_EXPERTS + N_GROUPS, :]
    g_max = jnp.max(g_logits, axis=0, keepdims=True)
    group_w = 1.0 / jnp.sum(jnp.exp(g_logits - g_max), axis=0, keepdims=True)
    sub8 = lax.broadcasted_iota(I32, (N_GROUPS, tm), 0)
    g_idx = jnp.min(jnp.where(g_logits == g_max, sub8, N_GROUPS), axis=0, keepdims=True)

    e_logits = logits[0:N_EXPERTS, :]
    erow = lax.broadcasted_iota(I32, (N_EXPERTS, tm), 0)
    neg_inf = -jnp.inf
    in_group = jnp.where((erow // EXPERTS_PER_GROUP) == g_idx, e_logits, neg_inf)
    m1 = jnp.max(in_group, axis=0, keepdims=True)
    idx1 = jnp.min(jnp.where(in_group == m1, erow, N_EXPERTS), axis=0, keepdims=True)
    rest = jnp.where(erow == idx1, neg_inf, in_group)
    m2 = jnp.max(rest, axis=0, keepdims=True)
    idx2 = jnp.min(jnp.where(rest == m2, erow, N_EXPERTS), axis=0, keepdims=True)
    r = jnp.exp(m2 - m1)
    w1 = 1.0 / (1.0 + r)
    gate1 = group_w * w1
    gate2 = group_w * (r * w1)

    oh1 = erow == idx1
    oh2 = erow == idx2
    picked = jnp.where(oh1 | oh2, 1.0, 0.0)
    tr = lax.broadcasted_iota(I32, (tm, tm), 0)
    tc = lax.broadcasted_iota(I32, (tm, tm), 1)
    before = jnp.where(tr < tc, 1.0, 0.0).astype(BF16)
    seen = jnp.dot(picked.astype(BF16), before, preferred_element_type=F32) + count_ref[:, 0:1]
    rank1 = jnp.sum(jnp.where(oh1, seen, 0.0), axis=0, keepdims=True)
    rank2 = jnp.sum(jnp.where(oh2, seen, 0.0), axis=0, keepdims=True)
    count_ref[...] = count_ref[...] + jnp.sum(picked, axis=1, keepdims=True)
    cnt_ref[...] = count_ref[...]

    sub = lax.broadcasted_iota(I32, (SUBLANES, tm), 0)
    rows = (idx1.astype(F32), idx2.astype(F32), rank1, rank2)
    route = jnp.zeros((SUBLANES, tm), F32)
    for n, val in enumerate(rows):
        route = jnp.where(sub == n, val, route)
    route_ref[...] = route

    sub_l = lax.broadcasted_iota(I32, (LANES, tm), 0)
    gates_t = jnp.where(sub_l == 0, gate1, jnp.where(sub_l == 1, gate2, 0.0))
    gcol_ref[...] = gates_t.T


def _post_mix(x2d, gdn_out, sb_out, w_out, ln, w_router_t, b_router):
    n_tok = x2d.shape[0]
    tm = min(POST_TM, n_tok)
    return pl.pallas_call(
        _post_mix_kernel,
        grid=(n_tok // tm,),
        in_specs=[
            pl.BlockSpec((tm, D_MODEL), lambda i: (i, 0)),
            pl.BlockSpec((tm, GDN_WIDTH), lambda i: (i, 0)),
            pl.BlockSpec((tm, SB_WIDTH), lambda i: (i, 0)),
            pl.BlockSpec((D_MODEL, D_MODEL), lambda i: (0, 0)),
            pl.BlockSpec((1, D_MODEL), lambda i: (0, 0)),
            pl.BlockSpec((LANES, D_MODEL), lambda i: (0, 0)),
            pl.BlockSpec((LANES, LANES), lambda i: (0, 0)),
        ],
        out_specs=[
            pl.BlockSpec((tm, D_MODEL), lambda i: (i, 0)),
            pl.BlockSpec((tm, ROW_TILES, LANES), lambda i: (i, 0, 0)),
            pl.BlockSpec((SUBLANES, tm), lambda i: (0, i)),
            pl.BlockSpec((tm, LANES), lambda i: (i, 0)),
            pl.BlockSpec((N_EXPERTS, LANES), lambda i: (0, 0)),
        ],
        out_shape=[
            jax.ShapeDtypeStruct((n_tok, D_MODEL), F32),
            jax.ShapeDtypeStruct((n_tok, ROW_TILES, LANES), F32),
            jax.ShapeDtypeStruct((SUBLANES, n_tok), F32),
            jax.ShapeDtypeStruct((n_tok, LANES), F32),
            jax.ShapeDtypeStruct((N_EXPERTS, LANES), F32),
        ],
        scratch_shapes=[pltpu.VMEM((N_EXPERTS, LANES), F32)],
        compiler_params=_cparams(1),
        name="post_mix",
    )(x2d, gdn_out, sb_out, w_out, ln, w_router_t, b_router)


def _dispatch_kernel(d1_ref, d2_ref, rows_ref, buf_in_hbm, buf_hbm, sem):
    del buf_in_hbm
    tm = rows_ref.shape[0]

    def issue(t, carry):
        pltpu.make_async_copy(rows_ref.at[t], buf_hbm.at[d1_ref[0, 0, t]], sem).start()
        pltpu.make_async_copy(rows_ref.at[t], buf_hbm.at[d2_ref[0, 0, t]], sem).start()
        return carry

    lax.fori_loop(0, tm, issue, 0)
    for _ in range(2):
        pltpu.make_async_copy(rows_ref, buf_hbm.at[pl.ds(0, tm)], sem).wait()


def _dispatch(dest1, dest2, h2r, n_rows):
    n_tok = h2r.shape[0]
    tm = min(DISP_TM, n_tok)
    n_steps = n_tok // tm
    d1 = dest1.reshape(n_steps, 1, tm)
    d2 = dest2.reshape(n_steps, 1, tm)
    smem_spec = pl.BlockSpec((1, 1, tm), lambda i: (i, 0, 0), memory_space=pltpu.SMEM)
    buf0 = jnp.zeros((n_rows, ROW_TILES, LANES), F32)
    return pl.pallas_call(
        _dispatch_kernel,
        grid=(n_steps,),
        in_specs=[smem_spec, smem_spec,
                  pl.BlockSpec((tm, ROW_TILES, LANES), lambda i: (i, 0, 0)),
                  pl.BlockSpec(memory_space=pl.ANY)],
        out_specs=pl.BlockSpec(memory_space=pl.ANY),
        out_shape=jax.ShapeDtypeStruct((n_rows, ROW_TILES, LANES), F32),
        scratch_shapes=[pltpu.SemaphoreType.DMA(())],
        input_output_aliases={3: 0},
        compiler_params=_cparams(1),
        name="moe_dispatch",
    )(d1, d2, h2r, buf0)


def _ffn_kernel(be_ref, x_ref, wg_ref, wu_ref, wd_ref, y_ref, wgu_bf, wd_bf):
    i = pl.program_id(0)

    @pl.when((i == 0) | (be_ref[i] != be_ref[jnp.maximum(i - 1, 0)]))
    def _():
        wgu_bf[:, :D_EXPERT] = wg_ref[0].astype(BF16)
        wgu_bf[:, D_EXPERT:] = wu_ref[0].astype(BF16)
        wd_bf[...] = wd_ref[0].astype(BF16)

    acc = jnp.zeros((MOE_BLOCK, 2 * D_EXPERT), F32)
    for j in range(ROW_TILES):
        acc = acc + jnp.dot(x_ref[:, j, :].astype(BF16), wgu_bf[j * LANES:(j + 1) * LANES, :],
                            preferred_element_type=F32)
    gate, up = acc[:, :D_EXPERT], acc[:, D_EXPERT:]
    hidden = (gate * jax.nn.sigmoid(gate)) * up
    y = jnp.dot(hidden.astype(BF16), wd_bf[...], preferred_element_type=F32)
    for j in range(ROW_TILES):
        y_ref[:, j, :] = y[:, j * LANES:(j + 1) * LANES]


def _ffn(block_expert, buf, w_gate, w_up, w_down):
    n_rows = buf.shape[0]
    n_blocks = n_rows // MOE_BLOCK
    grid_spec = pltpu.PrefetchScalarGridSpec(
        num_scalar_prefetch=1,
        grid=(n_blocks,),
        in_specs=[
            pl.BlockSpec((MOE_BLOCK, ROW_TILES, LANES), lambda i, be: (i, 0, 0)),
            pl.BlockSpec((1, D_MODEL, D_EXPERT), lambda i, be: (be[i], 0, 0)),
            pl.BlockSpec((1, D_MODEL, D_EXPERT), lambda i, be: (be[i], 0, 0)),
            pl.BlockSpec((1, D_EXPERT, D_MODEL), lambda i, be: (be[i], 0, 0)),
        ],
        out_specs=pl.BlockSpec((MOE_BLOCK, ROW_TILES, LANES), lambda i, be: (i, 0, 0)),
        scratch_shapes=[pltpu.VMEM((D_MODEL, 2 * D_EXPERT), BF16),
                        pltpu.VMEM((D_EXPERT, D_MODEL), BF16)],
    )
    return pl.pallas_call(
        _ffn_kernel,
        grid_spec=grid_spec,
        out_shape=jax.ShapeDtypeStruct((n_rows, ROW_TILES, LANES), F32),
        compiler_params=_cparams(1),
        name="moe_ffn",
    )(block_expert, buf, w_gate, w_up, w_down)


def _final_kernel(d1_ref, d2_ref, ybuf_hbm, x1_ref, gcol_ref, p_ref, wpg_ref, wpp_ref,
                  lnp_ref, lnf_ref, out_ref, y1_ref, y2_ref, sem):
    tm = x1_ref.shape[0]

    def issue(t, carry):
        pltpu.make_async_copy(ybuf_hbm.at[d1_ref[0, 0, t]], y1_ref.at[t], sem).start()
        pltpu.make_async_copy(ybuf_hbm.at[d2_ref[0, 0, t]], y2_ref.at[t], sem).start()
        return carry

    lax.fori_loop(0, tm, issue, 0)
    pltpu.make_async_copy(ybuf_hbm.at[pl.ds(0, tm)], y1_ref, sem).wait()
    pltpu.make_async_copy(ybuf_hbm.at[pl.ds(0, tm)], y2_ref, sem).wait()

    g1 = gcol_ref[:, 0:1]
    g2 = gcol_ref[:, 1:2]
    moe = jnp.concatenate([g1 * y1_ref[:, j, :] + g2 * y2_ref[:, j, :] for j in range(ROW_TILES)], axis=1)
    x2 = x1_ref[...] + moe
    hp = _rms(x2, lnp_ref[...]).astype(BF16)
    ple_gate = jax.nn.sigmoid(jnp.dot(hp, wpg_ref[...], preferred_element_type=F32))
    ple = jnp.dot(p_ref[...].astype(BF16), wpp_ref[...], preferred_element_type=F32)
    x3 = x2 + ple_gate * ple
    out_ref[...] = _rms(x3, lnf_ref[...])


def _final(dest1, dest2, ybuf, x1, gcol, p2d, w_ple_gate, w_ple_proj, ln_ple, ln_final):
    n_tok = x1.shape[0]
    tm = min(FINAL_TM, n_tok)
    n_steps = n_tok // tm
    d1 = dest1.reshape(n_steps, 1, tm)
    d2 = dest2.reshape(n_steps, 1, tm)
    smem_spec = pl.BlockSpec((1, 1, tm), lambda i: (i, 0, 0), memory_space=pltpu.SMEM)
    return pl.pallas_call(
        _final_kernel,
        grid=(n_steps,),
        in_specs=[
            smem_spec, smem_spec,
            pl.BlockSpec(memory_space=pl.ANY),
            pl.BlockSpec((tm, D_MODEL), lambda i: (i, 0)),
            pl.BlockSpec((tm, LANES), lambda i: (i, 0)),
            pl.BlockSpec((tm, PLE_DIM), lambda i: (i, 0)),
            pl.BlockSpec((D_MODEL, D_MODEL), lambda i: (0, 0)),
            pl.BlockSpec((PLE_DIM, D_MODEL), lambda i: (0, 0)),
            pl.BlockSpec((1, D_MODEL), lambda i: (0, 0)),
            pl.BlockSpec((1, D_MODEL), lambda i: (0, 0)),
        ],
        out_specs=pl.BlockSpec((tm, D_MODEL), lambda i: (i, 0)),
        out_shape=jax.ShapeDtypeStruct((n_tok, D_MODEL), F32),
        scratch_shapes=[pltpu.VMEM((tm, ROW_TILES, LANES), F32),
                        pltpu.VMEM((tm, ROW_TILES, LANES), F32),
                        pltpu.SemaphoreType.DMA(())],
        compiler_params=_cparams(1),
        name="moe_combine_ple_final",
    )(d1, d2, ybuf, x1, gcol, p2d, w_ple_gate, w_ple_proj, ln_ple, ln_final)


def _layer(x2d, p2d, bsz, seq, ln_mix, w_in, conv_w, a_log, dt_bias, gdn_norm_w, sb_norm_w, w_out,
           ln_moe, w_rg, b_rg, w_re, b_re, w_eg, w_eu, w_ed, ln_ple, w_ple_gate, w_ple_proj, ln_final):
    n_tok = bsz * seq
    C, H = GDN_CHUNK, GDN_HEADS
    o_z = 3 * GDN_WIDTH
    o_b = o_z + GDN_WIDTH
    o_q = o_b + 2 * H
    w_all = jnp.concatenate(
        [w_in[:, :o_b], w_in[:, o_q:], w_in[:, o_b:o_q],
         jnp.zeros((D_MODEL, LANES - 2 * H), F32)], axis=1).astype(BF16)
    proj, ba = _in_proj(x2d, ln_mix.reshape(1, D_MODEL), w_all)
    proj3 = proj.reshape(bsz, seq, PROJ_W)
    ba_rows = (ba[:, :2 * H].reshape(bsz, seq // C, C, 2, H)
               .transpose(0, 1, 3, 4, 2).reshape(bsz, seq // C, 2, GDN_HC))
    gparams = jnp.repeat(jnp.stack([a_log, dt_bias]).astype(F32), C, axis=1)

    gdn_out = _gdn(proj3, ba_rows, conv_w, gparams, gdn_norm_w.reshape(1, GDN_HEAD_DIM))
    sb_out = _sb(proj3, jnp.tile(sb_norm_w, LANES // SB_HEAD_DIM).reshape(1, LANES))

    w_router_t = jnp.zeros((LANES, D_MODEL), F32)
    w_router_t = w_router_t.at[:N_EXPERTS].set(w_re.T).at[N_EXPERTS:N_EXPERTS + N_GROUPS].set(w_rg.T)
    b_router = jnp.zeros((LANES,), F32).at[:N_EXPERTS].set(b_re).at[N_EXPERTS:N_EXPERTS + N_GROUPS].set(b_rg)
    b_router = jnp.broadcast_to(b_router[:, None], (LANES, LANES))
    x1, h2r, route, gcol, cnt = _post_mix(
        x2d, gdn_out.reshape(n_tok, GDN_WIDTH), sb_out.reshape(n_tok, SB_WIDTH),
        w_out.astype(BF16), ln_moe.reshape(1, D_MODEL), w_router_t, b_router)

    counts = cnt[:, 0].astype(I32)
    padded = (counts + MOE_BLOCK - 1) // MOE_BLOCK * MOE_BLOCK
    padded_end = jnp.cumsum(padded)
    padded_start = padded_end - padded
    route_i = route[:4].astype(I32)
    dest1 = padded_start[route_i[0]] + route_i[2]
    dest2 = padded_start[route_i[1]] + route_i[3]
    n_blocks = (2 * n_tok + MOE_BLOCK - 1) // MOE_BLOCK + N_EXPERTS
    block_row0 = jnp.arange(n_blocks, dtype=I32) * MOE_BLOCK
    block_expert = jnp.minimum(
        jnp.sum((padded_end[None, :] <= block_row0[:, None]).astype(I32), axis=1), N_EXPERTS - 1)

    buf = _dispatch(dest1, dest2, h2r, n_blocks * MOE_BLOCK)
    ybuf = _ffn(block_expert, buf, w_eg, w_eu, w_ed)
    return _final(dest1, dest2, ybuf, x1, gcol, p2d, w_ple_gate.astype(BF16), w_ple_proj.astype(BF16),
                  ln_ple.reshape(1, D_MODEL), ln_final.reshape(1, D_MODEL))


def kernel(x, p, ln_mix, w_in, conv_w, a_log, dt_bias, gdn_norm_w, sb_norm_w, w_out, ln_moe,
           w_router_group, b_router_group, w_router_expert, b_router_expert, w_expert_gate,
           w_expert_up, w_expert_down, ln_ple, w_ple_gate, w_ple_proj, ln_final):
    bsz, seq, _ = x.shape
    depth = p.shape[0]
    assert depth == 1, "the final RMSNorm is fused into the layer's last kernel"
    x2d = x.reshape(bsz * seq, D_MODEL)
    out = _layer(x2d, p[0].reshape(bsz * seq, PLE_DIM), bsz, seq, ln_mix[0], w_in[0], conv_w[0],
                 a_log[0], dt_bias[0], gdn_norm_w[0], sb_norm_w[0], w_out[0], ln_moe[0],
                 w_router_group[0], b_router_group[0], w_router_expert[0], b_router_expert[0],
                 w_expert_gate[0], w_expert_up[0], w_expert_down[0], ln_ple[0], w_ple_gate[0],
                 w_ple_proj[0], ln_final)
    return out.reshape(bsz, seq, D_MODEL)
```

```python
import jax
import jax.numpy as jnp
from jax import lax
from jax.experimental import pallas as pl
from jax.experimental.pallas import tpu as pltpu

F32 = jnp.float32
BF16 = jnp.bfloat16
I32 = jnp.int32

D_MODEL = 1024
PLE_DIM = 256
GDN_HEADS = 4
GDN_HEAD_DIM = 128
GDN_WIDTH = GDN_HEADS * GDN_HEAD_DIM
GDN_CONV = 4
GDN_CHUNK = 64
SB_HEADS = 8
SB_HEAD_DIM = 64
SB_WIDTH = SB_HEADS * SB_HEAD_DIM
N_GROUPS = 8
EXPERTS_PER_GROUP = 8
N_EXPERTS = N_GROUPS * EXPERTS_PER_GROUP
D_EXPERT = D_MODEL // 4
MOE_BLOCK = 128
EPS = 1e-6

LANES = 128
SUBLANES = 8
ROW_TILES = D_MODEL // LANES
PROJ_W = 3 * GDN_WIDTH + GDN_WIDTH + 3 * SB_WIDTH
PROJ_CHUNK = 512
PREV_ROWS = 16
VMEM_LIMIT = 48 * 1024 * 1024

IN_TM = 512
GDN_CHUNKS_PER_STEP = 4
GDN_HC = GDN_HEADS * GDN_CHUNK
SB_TILE = 256
POST_TM = 256
DISP_TM = 256
FINAL_TM = 256


def _cparams(n_axes):
    return pltpu.CompilerParams(dimension_semantics=("arbitrary",) * n_axes,
                                vmem_limit_bytes=VMEM_LIMIT)


def _rms(x, w):
    return x * lax.rsqrt(jnp.mean(x * x, axis=-1, keepdims=True) + EPS) * w


def _mm(a, b):
    return jnp.dot(a.astype(BF16), b.astype(BF16), preferred_element_type=F32)


def _mm_nt(a, b):
    return lax.dot_general(a.astype(BF16), b.astype(BF16), (((1,), (1,)), ((), ())),
                           preferred_element_type=F32)


def _mm_tn(a, b):
    return lax.dot_general(a.astype(BF16), b.astype(BF16), (((0,), (0,)), ((), ())),
                           preferred_element_type=F32)


def _split(a):
    hi = a.astype(BF16)
    lo = (a - hi.astype(F32)).astype(BF16)
    return hi, lo


def _softplus(x):
    return jnp.maximum(x, 0.0) + jnp.log(1.0 + jnp.exp(-jnp.abs(x)))


def _in_proj_kernel(x_ref, ln_ref, w_ref, proj_ref, ba_ref):
    h = _rms(x_ref[...], ln_ref[...]).astype(BF16)
    for j in range(0, PROJ_W, PROJ_CHUNK):
        proj_ref[:, j:j + PROJ_CHUNK] = jnp.dot(
            h, w_ref[:, j:j + PROJ_CHUNK], preferred_element_type=F32).astype(BF16)
    ba_ref[...] = jnp.dot(h, w_ref[:, PROJ_W:PROJ_W + LANES], preferred_element_type=F32)


def _in_proj(x2d, ln, w_all):
    n_tok = x2d.shape[0]
    tm = min(IN_TM, n_tok)
    return pl.pallas_call(
        _in_proj_kernel,
        grid=(n_tok // tm,),
        in_specs=[pl.BlockSpec((tm, D_MODEL), lambda i: (i, 0)),
                  pl.BlockSpec((1, D_MODEL), lambda i: (0, 0)),
                  pl.BlockSpec((D_MODEL, PROJ_W + LANES), lambda i: (0, 0))],
        out_specs=[pl.BlockSpec((tm, PROJ_W), lambda i: (i, 0)),
                   pl.BlockSpec((tm, LANES), lambda i: (i, 0))],
        out_shape=[jax.ShapeDtypeStruct((n_tok, PROJ_W), BF16),
                   jax.ShapeDtypeStruct((n_tok, LANES), F32)],
        compiler_params=_cparams(1),
        name="in_proj",
    )(x2d, ln, w_all)


def _gdn_kernel(cur_ref, prev_ref, z_ref, ba_ref, convw_ref, gp_ref, nw_ref, out_ref, state_ref):
    C, H, Dh, HC = GDN_CHUNK, GDN_HEADS, GDN_HEAD_DIM, GDN_HC
    rows_per_step = GDN_CHUNKS_PER_STEP * C
    step = pl.program_id(1)

    @pl.when(step == 0)
    def _():
        state_ref[...] = jnp.zeros_like(state_ref)

    cur = cur_ref[0].astype(F32)
    prev = jnp.where(step > 0, prev_ref[0].astype(F32), 0.0)
    stacked = jnp.concatenate([prev, cur], axis=0)
    w = convw_ref[...]
    y = w[GDN_CONV - 1:GDN_CONV, :] * cur
    for k in range(GDN_CONV - 1):
        shift = GDN_CONV - 1 - k
        y = y + w[k:k + 1, :] * pltpu.roll(stacked, shift, axis=0)[PREV_ROWS:, :]
    qkv = y * jax.nn.sigmoid(y)

    ri = lax.broadcasted_iota(I32, (HC, HC), 0)
    ci = lax.broadcasted_iota(I32, (HC, HC), 1)
    same_head = (ri // C) == (ci // C)
    incl = same_head & (ci <= ri)
    strict = same_head & (ci < ri)
    diag = ci == ri
    eye = jnp.where(diag, 1.0, 0.0).astype(F32)
    gp = gp_ref[...]

    def stack_heads(rows, base):
        return jnp.concatenate([rows[:, base + h * Dh:base + (h + 1) * Dh] for h in range(H)], axis=0)

    def scalars(n):
        ba = ba_ref[0, n]
        beta_row = jax.nn.sigmoid(ba[0:1, :])
        g_row = -jnp.exp(gp[0:1, :]) * _softplus(ba[1:2, :] + gp[1:2, :])
        gc_col = jnp.sum(jnp.where(incl, jnp.broadcast_to(g_row, (HC, HC)), 0.0), axis=1, keepdims=True)
        beta_col = jnp.sum(jnp.where(diag, jnp.broadcast_to(beta_row, (HC, HC)), 0.0), axis=1, keepdims=True)
        gc_row = jnp.sum(jnp.where(diag, jnp.broadcast_to(gc_col, (HC, HC)), 0.0), axis=0, keepdims=True)
        decay = jnp.where(incl, jnp.exp(jnp.where(incl, gc_col - gc_row, 0.0)), 0.0)
        return gc_col, beta_col, decay

    def normalized(n):
        rows = qkv[n * C:(n + 1) * C, :]
        q = stack_heads(rows, 0)
        k = stack_heads(rows, GDN_WIDTH)
        v = stack_heads(rows, 2 * GDN_WIDTH)
        q = q * lax.rsqrt(jnp.sum(q * q, axis=-1, keepdims=True) + EPS) * (Dh ** -0.5)
        k = k * lax.rsqrt(jnp.sum(k * k, axis=-1, keepdims=True) + EPS)
        return q, k, v

    chunk_ids = range(GDN_CHUNKS_PER_STEP)
    qkv_n = [normalized(n) for n in chunk_ids]
    sc = [scalars(n) for n in chunk_ids]
    kk = [_mm_nt(k, k) for _, k, _ in qkv_n]
    p = [jnp.where(strict, -(beta_col * kk_n * decay), 0.0) for (_, beta_col, decay), kk_n in zip(sc, kk)]
    t_inv = [eye + p_n for p_n in p]
    for _ in range(5):
        p = [_mm(p_n, p_n) for p_n in p]
        t_inv = [t_n + _mm(t_n, p_n) for t_n, p_n in zip(t_inv, p)]
    exp_gc = [jnp.exp(gc_col) for gc_col, _, _ in sc]
    sol = [_mm(t_n, jnp.concatenate([v * beta_col, k * (beta_col * e_n)], axis=1))
           for t_n, (_, k, v), (_, beta_col, _), e_n in zip(t_inv, qkv_n, sc, exp_gc)]
    qk = [_mm_nt(q, k) * decay for (q, k, _), (_, _, decay) in zip(qkv_n, sc)]
    pre = [dict(u=sol[n][:, :Dh], wk=sol[n][:, Dh:], qk=qk[n], qe=qkv_n[n][0] * exp_gc[n],
                k=qkv_n[n][1], gc=sc[n][0]) for n in chunk_ids]

    for n in chunk_ids:
        c = pre[n]
        v_new, o_state = [], []
        for h in range(H):
            rows = slice(h * C, (h + 1) * C)
            ws = _mm(jnp.concatenate([c["wk"][rows], c["qe"][rows]], axis=0), state_ref[h])
            v_new.append(c["u"][rows] - ws[:C])
            o_state.append(ws[C:])
        v_new = jnp.concatenate(v_new, axis=0)
        o = jnp.concatenate(o_state, axis=0) + _mm(c["qk"], v_new)
        for h in range(H):
            rows = slice(h * C, (h + 1) * C)
            g_last = c["gc"][(h + 1) * C - 1:(h + 1) * C, :]
            k_dec = c["k"][rows] * jnp.exp(g_last - c["gc"][rows])
            state_ref[h] = state_ref[h] * jnp.exp(g_last) + _mm_tn(k_dec, v_new[rows])
        o_n = _rms(o, nw_ref[...])
        for h in range(H):
            zz = z_ref[0, n * C:(n + 1) * C, h * Dh:(h + 1) * Dh].astype(F32)
            out_ref[0, n * C:(n + 1) * C, h * Dh:(h + 1) * Dh] = (
                o_n[h * C:(h + 1) * C] * (zz * jax.nn.sigmoid(zz))).astype(BF16)


def _gdn(proj3, ba_rows, conv_w, gparams, norm_w):
    bsz, seq, _ = proj3.shape
    cps = GDN_CHUNKS_PER_STEP
    rows = cps * GDN_CHUNK
    prev_per_step = rows // PREV_ROWS
    return pl.pallas_call(
        _gdn_kernel,
        grid=(bsz, seq // rows),
        in_specs=[
            pl.BlockSpec((1, rows, 3 * GDN_WIDTH), lambda b, s: (b, s, 0)),
            pl.BlockSpec((1, PREV_ROWS, 3 * GDN_WIDTH),
                         lambda b, s: (b, jnp.maximum(s * prev_per_step - 1, 0), 0)),
            pl.BlockSpec((1, rows, GDN_WIDTH), lambda b, s: (b, s, 3)),
            pl.BlockSpec((1, cps, 2, GDN_HC), lambda b, s: (b, s, 0, 0)),
            pl.BlockSpec((GDN_CONV, 3 * GDN_WIDTH), lambda b, s: (0, 0)),
            pl.BlockSpec((2, GDN_HC), lambda b, s: (0, 0)),
            pl.BlockSpec((1, GDN_HEAD_DIM), lambda b, s: (0, 0)),
        ],
        out_specs=pl.BlockSpec((1, rows, GDN_WIDTH), lambda b, s: (b, s, 0)),
        out_shape=jax.ShapeDtypeStruct((bsz, seq, GDN_WIDTH), BF16),
        scratch_shapes=[pltpu.VMEM((GDN_HEADS, GDN_HEAD_DIM, GDN_HEAD_DIM), F32)],
        compiler_params=_cparams(2),
        name="gdn",
    )(proj3, proj3, proj3, ba_rows, conv_w, gparams, norm_w)


def _sb_kernel(q_ref, k_ref, v_ref, nw_ref, out_ref, acc_ref, later_ref):
    T, Dh = SB_TILE, SB_HEAD_DIM
    qb = pl.program_id(2)
    row = lax.broadcasted_iota(I32, (T, T), 0)
    col = lax.broadcasted_iota(I32, (T, T), 1)
    causal = col < row
    suffix_ones = jnp.where(row >= col, 1.0, 0.0).astype(BF16)
    suffix2 = jnp.concatenate([suffix_ones, suffix_ones], axis=0)

    low = lax.broadcasted_iota(I32, (T, LANES), 1) < Dh
    q = q_ref[0] * jnp.asarray(Dh ** -0.5, BF16)
    zeros = jnp.zeros_like(q)
    q_heads = (jnp.where(low, q, zeros), jnp.where(low, zeros, q))

    def tiles(kbs, on_diagonal):
        blocks = []
        for kb in kbs:
            start = pl.multiple_of(kb * T, T)
            blocks.append((k_ref[0, pl.ds(start, T), :], v_ref[0, pl.ds(start, T), :]))
        chains = [(i, k_blk, v_blk) for i in range(2) for k_blk, v_blk in blocks]
        zs = [lax.dot_general(q_heads[i], k_blk, (((1,), (1,)), ((), ())), preferred_element_type=F32)
              for i, k_blk, _ in chains]
        splits = []
        for z in zs:
            log_remain = -_softplus(z)
            if on_diagonal:
                log_remain = jnp.where(causal, log_remain, 0.0)
            hi, lo = _split(log_remain)
            splits.append(jnp.concatenate([hi, lo], axis=1))
        incl_after = [jnp.dot(s, suffix2, preferred_element_type=F32) for s in splits]
        later = [later_ref[0], later_ref[1]]
        weights = []
        for (i, _, _), z, incl in zip(chains, zs, incl_after):
            w = jnp.exp(z + incl + later[i])
            if on_diagonal:
                w = jnp.where(causal, w, 0.0)
            weights.append(w.astype(BF16))
            later[i] = later[i] + incl[:, 0:1]
        acc = [acc_ref[0], acc_ref[1]]
        for (i, _, v_blk), w in zip(chains, weights):
            acc[i] = acc[i] + jnp.dot(w, v_blk, preferred_element_type=F32)
        for i in range(2):
            later_ref[i] = later[i]
            acc_ref[i] = acc[i]

    later_ref[...] = jnp.zeros_like(later_ref)
    acc_ref[...] = jnp.zeros_like(acc_ref)
    tiles([qb], True)

    def pair(i, carry):
        kb = qb - 1 - 2 * i
        tiles([kb, kb - 1], False)
        return carry

    lax.fori_loop(0, qb // 2, pair, 0)

    @pl.when(qb % 2 == 1)
    def _():
        tiles([0], False)

    o = jnp.where(low, acc_ref[0], acc_ref[1])
    hr = lax.broadcasted_iota(I32, (LANES, LANES), 0) // Dh
    hc = lax.broadcasted_iota(I32, (LANES, LANES), 1) // Dh
    avg = jnp.where(hr == hc, 1.0 / Dh, 0.0).astype(BF16)
    hi, lo = _split(o * o)
    ms = jnp.dot(jnp.concatenate([hi, lo], axis=1), jnp.concatenate([avg, avg], axis=0),
                 preferred_element_type=F32)
    out_ref[0] = (o * lax.rsqrt(ms + EPS) * nw_ref[...]).astype(BF16)


def _sb(proj3, norm_w2):
    bsz, seq, _ = proj3.shape
    tile = min(SB_TILE, seq)
    assert tile == SB_TILE and seq % tile == 0
    pairs = SB_WIDTH // LANES
    q_col = (3 * GDN_WIDTH + GDN_WIDTH) // LANES
    k_col = q_col + pairs
    v_col = k_col + pairs
    return pl.pallas_call(
        _sb_kernel,
        grid=(bsz, pairs, seq // tile),
        in_specs=[
            pl.BlockSpec((1, tile, LANES), lambda b, hp, qb: (b, qb, q_col + hp)),
            pl.BlockSpec((1, seq, LANES), lambda b, hp, qb: (b, 0, k_col + hp)),
            pl.BlockSpec((1, seq, LANES), lambda b, hp, qb: (b, 0, v_col + hp)),
            pl.BlockSpec((1, LANES), lambda b, hp, qb: (0, 0)),
        ],
        out_specs=pl.BlockSpec((1, tile, LANES), lambda b, hp, qb: (b, qb, hp)),
        out_shape=jax.ShapeDtypeStruct((bsz, seq, SB_WIDTH), BF16),
        scratch_shapes=[pltpu.VMEM((2, tile, LANES), F32),
                        pltpu.VMEM((2, tile, 1), F32)],
        compiler_params=_cparams(3),
        name="sb_attn",
    )(proj3, proj3, proj3, norm_w2)


def _post_mix_kernel(x_ref, g_ref, s_ref, wo_ref, ln_ref, wrt_ref, br_ref,
                     x1_ref, h2r_ref, route_ref, gcol_ref, cnt_ref, count_ref):
    tm = x_ref.shape[0]
    i = pl.program_id(0)

    @pl.when(i == 0)
    def _():
        count_ref[...] = jnp.zeros_like(count_ref)

    x1 = (x_ref[...]
          + jnp.dot(g_ref[...], wo_ref[0:GDN_WIDTH, :], preferred_element_type=F32)
          + jnp.dot(s_ref[...], wo_ref[GDN_WIDTH:, :], preferred_element_type=F32))
    x1_ref[...] = x1
    h2 = _rms(x1, ln_ref[...])
    for j in range(ROW_TILES):
        h2r_ref[:, j, :] = h2[:, j * LANES:(j + 1) * LANES]

    w_hi, w_lo = _split(wrt_ref[...])
    h_hi, h_lo = _split(h2)
    nt = (((1,), (1,)), ((), ()))
    logits = (lax.dot_general(w_hi, h_hi, nt, preferred_element_type=F32)
              + lax.dot_general(w_hi, h_lo, nt, preferred_element_type=F32)
              + lax.dot_general(w_lo, h_hi, nt, preferred_element_type=F32)) + br_ref[:, 0:1]

    g_logits = logits[N_EXPERTS:N_EXPERTS + N_GROUPS, :]
    g_max = jnp.max(g_logits, axis=0, keepdims=True)
    group_w = 1.0 / jnp.sum(jnp.exp(g_logits - g_max), axis=0, keepdims=True)
    sub8 = lax.broadcasted_iota(I32, (N_GROUPS, tm), 0)
    g_idx = jnp.min(jnp.where(g_logits == g_max, sub8, N_GROUPS), axis=0, keepdims=True)

    e_logits = logits[0:N_EXPERTS, :]
    erow = lax.broadcasted_iota(I32, (N_EXPERTS, tm), 0)
    neg_inf = -jnp.inf
    in_group = jnp.where((erow // EXPERTS_PER_GROUP) == g_idx, e_logits, neg_inf)
    m1 = jnp.max(in_group, axis=0, keepdims=True)
    idx1 = jnp.min(jnp.where(in_group == m1, erow, N_EXPERTS), axis=0, keepdims=True)
    rest = jnp.where(erow == idx1, neg_inf, in_group)
    m2 = jnp.max(rest, axis=0, keepdims=True)
    idx2 = jnp.min(jnp.where(rest == m2, erow, N_EXPERTS), axis=0, keepdims=True)
    r = jnp.exp(m2 - m1)
    w1 = 1.0 / (1.0 + r)
    gate1 = group_w * w1
    gate2 = group_w * (r * w1)

    oh1 = erow == idx1
    oh2 = erow == idx2
    picked = jnp.where(oh1 | oh2, 1.0, 0.0)
    tr = lax.broadcasted_iota(I32, (tm, tm), 0)
    tc = lax.broadcasted_iota(I32, (tm, tm), 1)
    before = jnp.where(tr < tc, 1.0, 0.0).astype(BF16)
    seen = jnp.dot(picked.astype(BF16), before, preferred_element_type=F32) + count_ref[:, 0:1]
    rank1 = jnp.sum(jnp.where(oh1, seen, 0.0), axis=0, keepdims=True)
    rank2 = jnp.sum(jnp.where(oh2, seen, 0.0), axis=0, keepdims=True)
    count_ref[...] = count_ref[...] + jnp.sum(picked, axis=1, keepdims=True)
    cnt_ref[...] = count_ref[...]

    sub = lax.broadcasted_iota(I32, (SUBLANES, tm), 0)
    rows = (idx1.astype(F32), idx2.astype(F32), rank1, rank2)
    route = jnp.zeros((SUBLANES, tm), F32)
    for n, val in enumerate(rows):
        route = jnp.where(sub == n, val, route)
    route_ref[...] = route

    sub_l = lax.broadcasted_iota(I32, (LANES, tm), 0)
    gates_t = jnp.where(sub_l == 0, gate1, jnp.where(sub_l == 1, gate2, 0.0))
    gcol_ref[...] = gates_t.T


def _post_mix(x2d, gdn_out, sb_out, w_out, ln, w_router_t, b_router):
    n_tok = x2d.shape[0]
    tm = min(POST_TM, n_tok)
    return pl.pallas_call(
        _post_mix_kernel,
        grid=(n_tok // tm,),
        in_specs=[
            pl.BlockSpec((tm, D_MODEL), lambda i: (i, 0)),
            pl.BlockSpec((tm, GDN_WIDTH), lambda i: (i, 0)),
            pl.BlockSpec((tm, SB_WIDTH), lambda i: (i, 0)),
            pl.BlockSpec((D_MODEL, D_MODEL), lambda i: (0, 0)),
            pl.BlockSpec((1, D_MODEL), lambda i: (0, 0)),
            pl.BlockSpec((LANES, D_MODEL), lambda i: (0, 0)),
            pl.BlockSpec((LANES, LANES), lambda i: (0, 0)),
        ],
        out_specs=[
            pl.BlockSpec((tm, D_MODEL), lambda i: (i, 0)),
            pl.BlockSpec((tm, ROW_TILES, LANES), lambda i: (i, 0, 0)),
            pl.BlockSpec((SUBLANES, tm), lambda i: (0, i)),
            pl.BlockSpec((tm, LANES), lambda i: (i, 0)),
            pl.BlockSpec((N_EXPERTS, LANES), lambda i: (0, 0)),
        ],
        out_shape=[
            jax.ShapeDtypeStruct((n_tok, D_MODEL), F32),
            jax.ShapeDtypeStruct((n_tok, ROW_TILES, LANES), F32),
            jax.ShapeDtypeStruct((SUBLANES, n_tok), F32),
            jax.ShapeDtypeStruct((n_tok, LANES), F32),
            jax.ShapeDtypeStruct((N_EXPERTS, LANES), F32),
        ],
        scratch_shapes=[pltpu.VMEM((N_EXPERTS, LANES), F32)],
        compiler_params=_cparams(1),
        name="post_mix",
    )(x2d, gdn_out, sb_out, w_out, ln, w_router_t, b_router)


def _dispatch_kernel(d1_ref, d2_ref, rows_ref, buf_in_hbm, buf_hbm, sem):
    del buf_in_hbm
    tm = rows_ref.shape[0]

    def issue(t, carry):
        pltpu.make_async_copy(rows_ref.at[t], buf_hbm.at[d1_ref[0, 0, t]], sem).start()
        pltpu.make_async_copy(rows_ref.at[t], buf_hbm.at[d2_ref[0, 0, t]], sem).start()
        return carry

    lax.fori_loop(0, tm, issue, 0)
    for _ in range(2):
        pltpu.make_async_copy(rows_ref, buf_hbm.at[pl.ds(0, tm)], sem).wait()


def _dispatch(dest1, dest2, h2r, n_rows):
    n_tok = h2r.shape[0]
    tm = min(DISP_TM, n_tok)
    n_steps = n_tok // tm
    d1 = dest1.reshape(n_steps, 1, tm)
    d2 = dest2.reshape(n_steps, 1, tm)
    smem_spec = pl.BlockSpec((1, 1, tm), lambda i: (i, 0, 0), memory_space=pltpu.SMEM)
    buf0 = jnp.zeros((n_rows, ROW_TILES, LANES), F32)
    return pl.pallas_call(
        _dispatch_kernel,
        grid=(n_steps,),
        in_specs=[smem_spec, smem_spec,
                  pl.BlockSpec((tm, ROW_TILES, LANES), lambda i: (i, 0, 0)),
                  pl.BlockSpec(memory_space=pl.ANY)],
        out_specs=pl.BlockSpec(memory_space=pl.ANY),
        out_shape=jax.ShapeDtypeStruct((n_rows, ROW_TILES, LANES), F32),
        scratch_shapes=[pltpu.SemaphoreType.DMA(())],
        input_output_aliases={3: 0},
        compiler_params=_cparams(1),
        name="moe_dispatch",
    )(d1, d2, h2r, buf0)


def _ffn_kernel(be_ref, x_ref, wg_ref, wu_ref, wd_ref, y_ref, wgu_bf, wd_bf):
    i = pl.program_id(0)

    @pl.when((i == 0) | (be_ref[i] != be_ref[jnp.maximum(i - 1, 0)]))
    def _():
        wgu_bf[:, :D_EXPERT] = wg_ref[0].astype(BF16)
        wgu_bf[:, D_EXPERT:] = wu_ref[0].astype(BF16)
        wd_bf[...] = wd_ref[0].astype(BF16)

    acc = jnp.zeros((MOE_BLOCK, 2 * D_EXPERT), F32)
    for j in range(ROW_TILES):
        acc = acc + jnp.dot(x_ref[:, j, :].astype(BF16), wgu_bf[j * LANES:(j + 1) * LANES, :],
                            preferred_element_type=F32)
    gate, up = acc[:, :D_EXPERT], acc[:, D_EXPERT:]
    hidden = (gate * jax.nn.sigmoid(gate)) * up
    y = jnp.dot(hidden.astype(BF16), wd_bf[...], preferred_element_type=F32)
    for j in range(ROW_TILES):
        y_ref[:, j, :] = y[:, j * LANES:(j + 1) * LANES]


def _ffn(block_expert, buf, w_gate, w_up, w_down):
    n_rows = buf.shape[0]
    n_blocks = n_rows // MOE_BLOCK
    grid_spec = pltpu.PrefetchScalarGridSpec(
        num_scalar_prefetch=1,
        grid=(n_blocks,),
        in_specs=[
            pl.BlockSpec((MOE_BLOCK, ROW_TILES, LANES), lambda i, be: (i, 0, 0)),
            pl.BlockSpec((1, D_MODEL, D_EXPERT), lambda i, be: (be[i], 0, 0)),
            pl.BlockSpec((1, D_MODEL, D_EXPERT), lambda i, be: (be[i], 0, 0)),
            pl.BlockSpec((1, D_EXPERT, D_MODEL), lambda i, be: (be[i], 0, 0)),
        ],
        out_specs=pl.BlockSpec((MOE_BLOCK, ROW_TILES, LANES), lambda i, be: (i, 0, 0)),
        scratch_shapes=[pltpu.VMEM((D_MODEL, 2 * D_EXPERT), BF16),
                        pltpu.VMEM((D_EXPERT, D_MODEL), BF16)],
    )
    return pl.pallas_call(
        _ffn_kernel,
        grid_spec=grid_spec,
        out_shape=jax.ShapeDtypeStruct((n_rows, ROW_TILES, LANES), F32),
        compiler_params=_cparams(1),
        name="moe_ffn",
    )(block_expert, buf, w_gate, w_up, w_down)


def _final_kernel(d1_ref, d2_ref, ybuf_hbm, x1_ref, gcol_ref, p_ref, wpg_ref, wpp_ref,
                  lnp_ref, lnf_ref, out_ref, y1_ref, y2_ref, sem):
    tm = x1_ref.shape[0]

    def issue(t, carry):
        pltpu.make_async_copy(ybuf_hbm.at[d1_ref[0, 0, t]], y1_ref.at[t], sem).start()
        pltpu.make_async_copy(ybuf_hbm.at[d2_ref[0, 0, t]], y2_ref.at[t], sem).start()
        return carry

    lax.fori_loop(0, tm, issue, 0)
    pltpu.make_async_copy(ybuf_hbm.at[pl.ds(0, tm)], y1_ref, sem).wait()
    pltpu.make_async_copy(ybuf_hbm.at[pl.ds(0, tm)], y2_ref, sem).wait()

    g1 = gcol_ref[:, 0:1]
    g2 = gcol_ref[:, 1:2]
    moe = jnp.concatenate([g1 * y1_ref[:, j, :] + g2 * y2_ref[:, j, :] for j in range(ROW_TILES)], axis=1)
    x2 = x1_ref[...] + moe
    hp = _rms(x2, lnp_ref[...]).astype(BF16)
    ple_gate = jax.nn.sigmoid(jnp.dot(hp, wpg_ref[...], preferred_element_type=F32))
    ple = jnp.dot(p_ref[...].astype(BF16), wpp_ref[...], preferred_element_type=F32)
    x3 = x2 + ple_gate * ple
    out_ref[...] = _rms(x3, lnf_ref[...])


def _final(dest1, dest2, ybuf, x1, gcol, p2d, w_ple_gate, w_ple_proj, ln_ple, ln_final):
    n_tok = x1.shape[0]
    tm = min(FINAL_TM, n_tok)
    n_steps = n_tok // tm
    d1 = dest1.reshape(n_steps, 1, tm)
    d2 = dest2.reshape(n_steps, 1, tm)
    smem_spec = pl.BlockSpec((1, 1, tm), lambda i: (i, 0, 0), memory_space=pltpu.SMEM)
    return pl.pallas_call(
        _final_kernel,
        grid=(n_steps,),
        in_specs=[
            smem_spec, smem_spec,
            pl.BlockSpec(memory_space=pl.ANY),
            pl.BlockSpec((tm, D_MODEL), lambda i: (i, 0)),
            pl.BlockSpec((tm, LANES), lambda i: (i, 0)),
            pl.BlockSpec((tm, PLE_DIM), lambda i: (i, 0)),
            pl.BlockSpec((D_MODEL, D_MODEL), lambda i: (0, 0)),
            pl.BlockSpec((PLE_DIM, D_MODEL), lambda i: (0, 0)),
            pl.BlockSpec((1, D_MODEL), lambda i: (0, 0)),
            pl.BlockSpec((1, D_MODEL), lambda i: (0, 0)),
        ],
        out_specs=pl.BlockSpec((tm, D_MODEL), lambda i: (i, 0)),
        out_shape=jax.ShapeDtypeStruct((n_tok, D_MODEL), F32),
        scratch_shapes=[pltpu.VMEM((tm, ROW_TILES, LANES), F32),
                        pltpu.VMEM((tm, ROW_TILES, LANES), F32),
                        pltpu.SemaphoreType.DMA(())],
        compiler_params=_cparams(1),
        name="moe_combine_ple_final",
    )(d1, d2, ybuf, x1, gcol, p2d, w_ple_gate, w_ple_proj, ln_ple, ln_final)


def _layer(x2d, p2d, bsz, seq, ln_mix, w_in, conv_w, a_log, dt_bias, gdn_norm_w, sb_norm_w, w_out,
           ln_moe, w_rg, b_rg, w_re, b_re, w_eg, w_eu, w_ed, ln_ple, w_ple_gate, w_ple_proj, ln_final):
    n_tok = bsz * seq
    C, H = GDN_CHUNK, GDN_HEADS
    o_z = 3 * GDN_WIDTH
    o_b = o_z + GDN_WIDTH
    o_q = o_b + 2 * H
    w_all = jnp.concatenate(
        [w_in[:, :o_b], w_in[:, o_q:], w_in[:, o_b:o_q],
         jnp.zeros((D_MODEL, LANES - 2 * H), F32)], axis=1).astype(BF16)
    proj, ba = _in_proj(x2d, ln_mix.reshape(1, D_MODEL), w_all)
    proj3 = proj.reshape(bsz, seq, PROJ_W)
    ba_rows = (ba[:, :2 * H].reshape(bsz, seq // C, C, 2, H)
               .transpose(0, 1, 3, 4, 2).reshape(bsz, seq // C, 2, GDN_HC))
    gparams = jnp.repeat(jnp.stack([a_log, dt_bias]).astype(F32), C, axis=1)

    gdn_out = _gdn(proj3, ba_rows, conv_w, gparams, gdn_norm_w.reshape(1, GDN_HEAD_DIM))
    sb_out = _sb(proj3, jnp.tile(sb_norm_w, LANES // SB_HEAD_DIM).reshape(1, LANES))

    w_router_t = jnp.zeros((LANES, D_MODEL), F32)
    w_router_t = w_router_t.at[:N_EXPERTS].set(w_re.T).at[N_EXPERTS:N_EXPERTS + N_GROUPS].set(w_rg.T)
    b_router = jnp.zeros((LANES,), F32).at[:N_EXPERTS].set(b_re).at[N_EXPERTS:N_EXPERTS + N_GROUPS].set(b_rg)
    b_router = jnp.broadcast_to(b_router[:, None], (LANES, LANES))
    x1, h2r, route, gcol, cnt = _post_mix(
        x2d, gdn_out.reshape(n_tok, GDN_WIDTH), sb_out.reshape(n_tok, SB_WIDTH),
        w_out.astype(BF16), ln_moe.reshape(1, D_MODEL), w_router_t, b_router)

    counts = cnt[:, 0].astype(I32)
    padded = (counts + MOE_BLOCK - 1) // MOE_BLOCK * MOE_BLOCK
    padded_end = jnp.cumsum(padded)
    padded_start = padded_end - padded
    route_i = route[:4].astype(I32)
    dest1 = padded_start[route_i[0]] + route_i[2]
    dest2 = padded_start[route_i[1]] + route_i[3]
    n_blocks = (2 * n_tok + MOE_BLOCK - 1) // MOE_BLOCK + N_EXPERTS
    block_row0 = jnp.arange(n_blocks, dtype=I32) * MOE_BLOCK
    block_expert = jnp.minimum(
        jnp.sum((padded_end[:, None] <= block_row0[None, :]).astype(I32), axis=0), N_EXPERTS - 1)

    buf = _dispatch(dest1, dest2, h2r, n_blocks * MOE_BLOCK)
    ybuf = _ffn(block_expert, buf, w_eg, w_eu, w_ed)
    return _final(dest1, dest2, ybuf, x1, gcol, p2d, w_ple_gate.astype(BF16), w_ple_proj.astype(BF16),
                  ln_ple.reshape(1, D_MODEL), ln_final.reshape(1, D_MODEL))


def kernel(x, p, ln_mix, w_in, conv_w, a_log, dt_bias, gdn_norm_w, sb_norm_w, w_out, ln_moe,
           w_router_group, b_router_group, w_router_expert, b_router_expert, w_expert_gate,
           w_expert_up, w_expert_down, ln_ple, w_ple_gate, w_ple_proj, ln_final):
    bsz, seq, _ = x.shape
    depth = p.shape[0]
    assert depth == 1, "the final RMSNorm is fused into the layer's last kernel"
    x2d = x.reshape(bsz * seq, D_MODEL)
    out = _layer(x2d, p[0].reshape(bsz * seq, PLE_DIM), bsz, seq, ln_mix[0], w_in[0], conv_w[0],
                 a_log[0], dt_bias[0], gdn_norm_w[0], sb_norm_w[0], w_out[0], ln_moe[0],
                 w_router_group[0], b_router_group[0], w_router_expert[0], b_router_expert[0],
                 w_expert_gate[0], w_expert_up[0], w_expert_down[0], ln_ple[0], w_ple_gate[0],
                 w_ple_proj[0], ln_final)
    return out.reshape(bsz, seq, D_MODEL)
```

```python
import jax
import jax.numpy as jnp
from jax import lax
from jax.experimental import pallas as pl
from jax.experimental.pallas import tpu as pltpu

F32 = jnp.float32
BF16 = jnp.bfloat16
I32 = jnp.int32

D_MODEL = 1024
PLE_DIM = 256
GDN_HEADS = 4
GDN_HEAD_DIM = 128
GDN_WIDTH = GDN_HEADS * GDN_HEAD_DIM
GDN_CONV = 4
GDN_CHUNK = 64
SB_HEADS = 8
SB_HEAD_DIM = 64
SB_WIDTH = SB_HEADS * SB_HEAD_DIM
N_GROUPS = 8
EXPERTS_PER_GROUP = 8
N_EXPERTS = N_GROUPS * EXPERTS_PER_GROUP
D_EXPERT = D_MODEL // 4
MOE_BLOCK = 128
EPS = 1e-6

LANES = 128
SUBLANES = 8
ROW_TILES = D_MODEL // LANES
PROJ_W = 3 * GDN_WIDTH + GDN_WIDTH + 3 * SB_WIDTH
PROJ_CHUNK = 512
PREV_ROWS = 16
VMEM_LIMIT = 48 * 1024 * 1024

IN_TM = 512
GDN_CHUNKS_PER_STEP = 4
GDN_HC = GDN_HEADS * GDN_CHUNK
SB_TILE = 256
POST_TM = 256
DISP_TM = 256
FINAL_TM = 256


def _cparams(n_axes):
    return pltpu.CompilerParams(dimension_semantics=("arbitrary",) * n_axes,
                                vmem_limit_bytes=VMEM_LIMIT)


def _rms(x, w):
    return x * lax.rsqrt(jnp.mean(x * x, axis=-1, keepdims=True) + EPS) * w


def _mm(a, b):
    return jnp.dot(a.astype(BF16), b.astype(BF16), preferred_element_type=F32)


def _mm_nt(a, b):
    return lax.dot_general(a.astype(BF16), b.astype(BF16), (((1,), (1,)), ((), ())),
                           preferred_element_type=F32)


def _mm_tn(a, b):
    return lax.dot_general(a.astype(BF16), b.astype(BF16), (((0,), (0,)), ((), ())),
                           preferred_element_type=F32)


def _split(a):
    hi = a.astype(BF16)
    lo = (a - hi.astype(F32)).astype(BF16)
    return hi, lo


def _softplus(x):
    return jnp.maximum(x, 0.0) + jnp.log(1.0 + jnp.exp(-jnp.abs(x)))


def _tiles_to_rows(x3):
    xt = pltpu.einshape("tjc->jtc", x3)
    return jnp.concatenate([xt[j] for j in range(ROW_TILES)], axis=1)


def _rows_to_tiles(x2):
    xt = jnp.stack([x2[:, j * LANES:(j + 1) * LANES] for j in range(ROW_TILES)], axis=0)
    return pltpu.einshape("jtc->tjc", xt)


def _in_proj_kernel(x_ref, ln_ref, w_ref, proj_ref, ba_ref):
    h = _rms(x_ref[...], ln_ref[...]).astype(BF16)
    for j in range(0, PROJ_W, PROJ_CHUNK):
        proj_ref[:, j:j + PROJ_CHUNK] = jnp.dot(
            h, w_ref[:, j:j + PROJ_CHUNK], preferred_element_type=F32).astype(BF16)
    ba_ref[...] = jnp.dot(h, w_ref[:, PROJ_W:PROJ_W + LANES], preferred_element_type=F32)


def _in_proj(x2d, ln, w_all):
    n_tok = x2d.shape[0]
    tm = min(IN_TM, n_tok)
    return pl.pallas_call(
        _in_proj_kernel,
        grid=(n_tok // tm,),
        in_specs=[pl.BlockSpec((tm, D_MODEL), lambda i: (i, 0)),
                  pl.BlockSpec((1, D_MODEL), lambda i: (0, 0)),
                  pl.BlockSpec((D_MODEL, PROJ_W + LANES), lambda i: (0, 0))],
        out_specs=[pl.BlockSpec((tm, PROJ_W), lambda i: (i, 0)),
                   pl.BlockSpec((tm, LANES), lambda i: (i, 0))],
        out_shape=[jax.ShapeDtypeStruct((n_tok, PROJ_W), BF16),
                   jax.ShapeDtypeStruct((n_tok, LANES), F32)],
        compiler_params=_cparams(1),
        name="in_proj",
    )(x2d, ln, w_all)


def _gdn_kernel(cur_ref, prev_ref, z_ref, ba_ref, convw_ref, gp_ref, nw_ref, out_ref, state_ref):
    C, H, Dh, HC = GDN_CHUNK, GDN_HEADS, GDN_HEAD_DIM, GDN_HC
    rows_per_step = GDN_CHUNKS_PER_STEP * C
    step = pl.program_id(1)

    @pl.when(step == 0)
    def _():
        state_ref[...] = jnp.zeros_like(state_ref)

    cur = cur_ref[0].astype(F32)
    prev = jnp.where(step > 0, prev_ref[0].astype(F32), 0.0)
    stacked = jnp.concatenate([prev, cur], axis=0)
    w = convw_ref[...]
    y = w[GDN_CONV - 1:GDN_CONV, :] * cur
    for k in range(GDN_CONV - 1):
        shift = GDN_CONV - 1 - k
        y = y + w[k:k + 1, :] * pltpu.roll(stacked, shift, axis=0)[PREV_ROWS:, :]
    qkv = y * jax.nn.sigmoid(y)

    ri = lax.broadcasted_iota(I32, (HC, HC), 0)
    ci = lax.broadcasted_iota(I32, (HC, HC), 1)
    same_head = (ri // C) == (ci // C)
    incl = same_head & (ci <= ri)
    strict = same_head & (ci < ri)
    diag = ci == ri
    eye = jnp.where(diag, 1.0, 0.0).astype(F32)
    gp = gp_ref[...]

    def stack_heads(rows, base):
        return jnp.concatenate([rows[:, base + h * Dh:base + (h + 1) * Dh] for h in range(H)], axis=0)

    def scalars(n):
        ba = ba_ref[0, n]
        beta_row = jax.nn.sigmoid(ba[0:1, :])
        g_row = -jnp.exp(gp[0:1, :]) * _softplus(ba[1:2, :] + gp[1:2, :])
        gc_col = jnp.sum(jnp.where(incl, jnp.broadcast_to(g_row, (HC, HC)), 0.0), axis=1, keepdims=True)
        beta_col = jnp.sum(jnp.where(diag, jnp.broadcast_to(beta_row, (HC, HC)), 0.0), axis=1, keepdims=True)
        gc_row = jnp.sum(jnp.where(diag, jnp.broadcast_to(gc_col, (HC, HC)), 0.0), axis=0, keepdims=True)
        decay = jnp.where(incl, jnp.exp(jnp.where(incl, gc_col - gc_row, 0.0)), 0.0)
        return gc_col, beta_col, decay

    def normalized(n):
        rows = qkv[n * C:(n + 1) * C, :]
        q = stack_heads(rows, 0)
        k = stack_heads(rows, GDN_WIDTH)
        v = stack_heads(rows, 2 * GDN_WIDTH)
        q = q * lax.rsqrt(jnp.sum(q * q, axis=-1, keepdims=True) + EPS) * (Dh ** -0.5)
        k = k * lax.rsqrt(jnp.sum(k * k, axis=-1, keepdims=True) + EPS)
        return q, k, v

    chunk_ids = range(GDN_CHUNKS_PER_STEP)
    qkv_n = [normalized(n) for n in chunk_ids]
    sc = [scalars(n) for n in chunk_ids]
    kk = [_mm_nt(k, k) for _, k, _ in qkv_n]
    p = [jnp.where(strict, -(beta_col * kk_n * decay), 0.0) for (_, beta_col, decay), kk_n in zip(sc, kk)]
    t_inv = [eye + p_n for p_n in p]
    for _ in range(5):
        p = [_mm(p_n, p_n) for p_n in p]
        t_inv = [t_n + _mm(t_n, p_n) for t_n, p_n in zip(t_inv, p)]
    exp_gc = [jnp.exp(gc_col) for gc_col, _, _ in sc]
    sol = [_mm(t_n, jnp.concatenate([v * beta_col, k * (beta_col * e_n)], axis=1))
           for t_n, (_, k, v), (_, beta_col, _), e_n in zip(t_inv, qkv_n, sc, exp_gc)]
    qk = [_mm_nt(q, k) * decay for (q, k, _), (_, _, decay) in zip(qkv_n, sc)]
    pre = [dict(u=sol[n][:, :Dh], wk=sol[n][:, Dh:], qk=qk[n], qe=qkv_n[n][0] * exp_gc[n],
                k=qkv_n[n][1], gc=sc[n][0]) for n in chunk_ids]

    for n in chunk_ids:
        c = pre[n]
        v_new, o_state = [], []
        for h in range(H):
            rows = slice(h * C, (h + 1) * C)
            ws = _mm(jnp.concatenate([c["wk"][rows], c["qe"][rows]], axis=0), state_ref[h])
            v_new.append(c["u"][rows] - ws[:C])
            o_state.append(ws[C:])
        v_new = jnp.concatenate(v_new, axis=0)
        o = jnp.concatenate(o_state, axis=0) + _mm(c["qk"], v_new)
        for h in range(H):
            rows = slice(h * C, (h + 1) * C)
            g_last = c["gc"][(h + 1) * C - 1:(h + 1) * C, :]
            k_dec = c["k"][rows] * jnp.exp(g_last - c["gc"][rows])
            state_ref[h] = state_ref[h] * jnp.exp(g_last) + _mm_tn(k_dec, v_new[rows])
        o_n = _rms(o, nw_ref[...])
        for h in range(H):
            zz = z_ref[0, n * C:(n + 1) * C, h * Dh:(h + 1) * Dh].astype(F32)
            out_ref[0, n * C:(n + 1) * C, h * Dh:(h + 1) * Dh] = (
                o_n[h * C:(h + 1) * C] * (zz * jax.nn.sigmoid(zz))).astype(BF16)


def _gdn(proj3, ba_rows, conv_w, gparams, norm_w):
    bsz, seq, _ = proj3.shape
    cps = GDN_CHUNKS_PER_STEP
    rows = cps * GDN_CHUNK
    prev_per_step = rows // PREV_ROWS
    return pl.pallas_call(
        _gdn_kernel,
        grid=(bsz, seq // rows),
        in_specs=[
            pl.BlockSpec((1, rows, 3 * GDN_WIDTH), lambda b, s: (b, s, 0)),
            pl.BlockSpec((1, PREV_ROWS, 3 * GDN_WIDTH),
                         lambda b, s: (b, jnp.maximum(s * prev_per_step - 1, 0), 0)),
            pl.BlockSpec((1, rows, GDN_WIDTH), lambda b, s: (b, s, 3)),
            pl.BlockSpec((1, cps, 2, GDN_HC), lambda b, s: (b, s, 0, 0)),
            pl.BlockSpec((GDN_CONV, 3 * GDN_WIDTH), lambda b, s: (0, 0)),
            pl.BlockSpec((2, GDN_HC), lambda b, s: (0, 0)),
            pl.BlockSpec((1, GDN_HEAD_DIM), lambda b, s: (0, 0)),
        ],
        out_specs=pl.BlockSpec((1, rows, GDN_WIDTH), lambda b, s: (b, s, 0)),
        out_shape=jax.ShapeDtypeStruct((bsz, seq, GDN_WIDTH), BF16),
        scratch_shapes=[pltpu.VMEM((GDN_HEADS, GDN_HEAD_DIM, GDN_HEAD_DIM), F32)],
        compiler_params=_cparams(2),
        name="gdn",
    )(proj3, proj3, proj3, ba_rows, conv_w, gparams, norm_w)


def _sb_kernel(q_ref, k_ref, v_ref, nw_ref, out_ref, acc_ref, later_ref):
    T, Dh = SB_TILE, SB_HEAD_DIM
    qb = pl.program_id(2)
    row = lax.broadcasted_iota(I32, (T, T), 0)
    col = lax.broadcasted_iota(I32, (T, T), 1)
    causal = col < row
    neg_suffix = jnp.where(row >= col, -1.0, 0.0).astype(BF16)
    neg_suffix2 = jnp.concatenate([neg_suffix, neg_suffix], axis=0)

    low = lax.broadcasted_iota(I32, (T, LANES), 1) < Dh
    q = q_ref[0] * jnp.asarray(Dh ** -0.5, BF16)
    zeros = jnp.zeros_like(q)
    q_heads = (jnp.where(low, q, zeros), jnp.where(low, zeros, q))

    def tiles(kbs, on_diagonal):
        blocks = []
        for kb in kbs:
            start = pl.multiple_of(kb * T, T)
            blocks.append((k_ref[0, pl.ds(start, T), :], v_ref[0, pl.ds(start, T), :]))
        chains = [(i, k_blk, v_blk) for i in range(2) for k_blk, v_blk in blocks]
        zs = [lax.dot_general(q_heads[i], k_blk, (((1,), (1,)), ((), ())), preferred_element_type=F32)
              for i, k_blk, _ in chains]
        splits = []
        for z in zs:
            neg_log_remain = _softplus(z)
            if on_diagonal:
                neg_log_remain = jnp.where(causal, neg_log_remain, 0.0)
            splits.append(neg_log_remain.astype(BF16))
        incl_after = [jnp.dot(s, neg_suffix, preferred_element_type=F32) for s in splits]
        later = [later_ref[0], later_ref[1]]
        weights = []
        for (i, _, _), z, incl in zip(chains, zs, incl_after):
            w = jnp.exp(z + incl + later[i])
            if on_diagonal:
                w = jnp.where(causal, w, 0.0)
            weights.append(w.astype(BF16))
            later[i] = later[i] + incl[:, 0:1]
        acc = [acc_ref[0], acc_ref[1]]
        for (i, _, v_blk), w in zip(chains, weights):
            acc[i] = acc[i] + jnp.dot(w, v_blk, preferred_element_type=F32)
        for i in range(2):
            later_ref[i] = later[i]
            acc_ref[i] = acc[i]

    later_ref[...] = jnp.zeros_like(later_ref)
    acc_ref[...] = jnp.zeros_like(acc_ref)
    tiles([qb], True)

    def quad(i, carry):
        kb = qb - 1 - 4 * i
        tiles([kb, kb - 1, kb - 2, kb - 3], False)
        return carry

    lax.fori_loop(0, qb // 4, quad, 0)
    left = qb % 4

    @pl.when(left >= 2)
    def _():
        tiles([left - 1, left - 2], False)

    @pl.when(left % 2 == 1)
    def _():
        tiles([0], False)

    o = jnp.where(low, acc_ref[0], acc_ref[1])
    hr = lax.broadcasted_iota(I32, (LANES, LANES), 0) // Dh
    hc = lax.broadcasted_iota(I32, (LANES, LANES), 1) // Dh
    avg = jnp.where(hr == hc, 1.0 / Dh, 0.0).astype(BF16)
    hi, lo = _split(o * o)
    ms = jnp.dot(jnp.concatenate([hi, lo], axis=1), jnp.concatenate([avg, avg], axis=0),
                 preferred_element_type=F32)
    out_ref[0] = (o * lax.rsqrt(ms + EPS) * nw_ref[...]).astype(BF16)


def _sb(proj3, norm_w2):
    bsz, seq, _ = proj3.shape
    tile = min(SB_TILE, seq)
    assert tile == SB_TILE and seq % tile == 0
    pairs = SB_WIDTH // LANES
    q_col = (3 * GDN_WIDTH + GDN_WIDTH) // LANES
    k_col = q_col + pairs
    v_col = k_col + pairs
    return pl.pallas_call(
        _sb_kernel,
        grid=(bsz, pairs, seq // tile),
        in_specs=[
            pl.BlockSpec((1, tile, LANES), lambda b, hp, qb: (b, qb, q_col + hp)),
            pl.BlockSpec((1, seq, LANES), lambda b, hp, qb: (b, 0, k_col + hp)),
            pl.BlockSpec((1, seq, LANES), lambda b, hp, qb: (b, 0, v_col + hp)),
            pl.BlockSpec((1, LANES), lambda b, hp, qb: (0, 0)),
        ],
        out_specs=pl.BlockSpec((1, tile, LANES), lambda b, hp, qb: (b, qb, hp)),
        out_shape=jax.ShapeDtypeStruct((bsz, seq, SB_WIDTH), BF16),
        scratch_shapes=[pltpu.VMEM((2, tile, LANES), F32),
                        pltpu.VMEM((2, tile, 1), F32)],
        compiler_params=_cparams(3),
        name="sb_attn",
    )(proj3, proj3, proj3, norm_w2)


def _post_mix_kernel(x_ref, g_ref, s_ref, wo_ref, ln_ref, wrt_ref, br_ref,
                     x1_ref, h2r_ref, route_ref, gcol_ref, cnt_ref, count_ref):
    tm = x_ref.shape[0]
    i = pl.program_id(0)

    @pl.when(i == 0)
    def _():
        count_ref[...] = jnp.zeros_like(count_ref)

    x1 = (x_ref[...]
          + jnp.dot(g_ref[...], wo_ref[0:GDN_WIDTH, :], preferred_element_type=F32)
          + jnp.dot(s_ref[...], wo_ref[GDN_WIDTH:, :], preferred_element_type=F32))
    x1_ref[...] = x1
    h2 = _rms(x1, ln_ref[...])
    h2r_ref[...] = _rows_to_tiles(h2)

    w_hi, w_lo = _split(wrt_ref[...])
    h_hi, h_lo = _split(h2)
    nt = (((1,), (1,)), ((), ()))
    logits = (lax.dot_general(w_hi, h_hi, nt, preferred_element_type=F32)
              + lax.dot_general(w_hi, h_lo, nt, preferred_element_type=F32)
              + lax.dot_general(w_lo, h_hi, nt, preferred_element_type=F32)) + br_ref[:, 0:1]

    g_logits = logits[N_EXPERTS:N_EXPERTS + N_GROUPS, :]
    g_max = jnp.max(g_logits, axis=0, keepdims=True)
    group_w = 1.0 / jnp.sum(jnp.exp(g_logits - g_max), axis=0, keepdims=True)
    sub8 = lax.broadcasted_iota(I32, (N_GROUPS, tm), 0)
    g_idx = jnp.min(jnp.where(g_logits == g_max, sub8, N_GROUPS), axis=0, keepdims=True)

    e_logits = logits[0:N_EXPERTS, :]
    erow = lax.broadcasted_iota(I32, (N_EXPERTS, tm), 0)
    neg_inf = -jnp.inf
    in_group = jnp.where((erow // EXPERTS_PER_GROUP) == g_idx, e_logits, neg_inf)
    m1 = jnp.max(in_group, axis=0, keepdims=True)
    idx1 = jnp.min(jnp.where(in_group == m1, erow, N_EXPERTS), axis=0, keepdims=True)
    rest = jnp.where(erow == idx1, neg_inf, in_group)
    m2 = jnp.max(rest, axis=0, keepdims=True)
    idx2 = jnp.min(jnp.where(rest == m2, erow, N_EXPERTS), axis=0, keepdims=True)
    r = jnp.exp(m2 - m1)
    w1 = 1.0 / (1.0 + r)
    gate1 = group_w * w1
    gate2 = group_w * (r * w1)

    oh1 = erow == idx1
    oh2 = erow == idx2
    picked = jnp.where(oh1 | oh2, 1.0, 0.0)
    tr = lax.broadcasted_iota(I32, (tm, tm), 0)
    tc = lax.broadcasted_iota(I32, (tm, tm), 1)
    before = jnp.where(tr < tc, 1.0, 0.0).astype(BF16)
    seen = jnp.dot(picked.astype(BF16), before, preferred_element_type=F32) + count_ref[:, 0:1]
    rank1 = jnp.sum(jnp.where(oh1, seen, 0.0), axis=0, keepdims=True)
    rank2 = jnp.sum(jnp.where(oh2, seen, 0.0), axis=0, keepdims=True)
    count_ref[...] = count_ref[...] + jnp.sum(picked, axis=1, keepdims=True)
    cnt_ref[...] = count_ref[...]

    sub = lax.broadcasted_iota(I32, (SUBLANES, tm), 0)
    rows = (idx1.astype(F32), idx2.astype(F32), rank1, rank2)
    route = jnp.zeros((SUBLANES, tm), F32)
    for n, val in enumerate(rows):
        route = jnp.where(sub == n, val, route)
    route_ref[...] = route

    sub_l = lax.broadcasted_iota(I32, (LANES, tm), 0)
    gates_t = jnp.where(sub_l == 0, gate1, jnp.where(sub_l == 1, gate2, 0.0))
    gcol_ref[...] = gates_t.T


def _post_mix(x2d, gdn_out, sb_out, w_out, ln, w_router_t, b_router):
    n_tok = x2d.shape[0]
    tm = min(POST_TM, n_tok)
    return pl.pallas_call(
        _post_mix_kernel,
        grid=(n_tok // tm,),
        in_specs=[
            pl.BlockSpec((tm, D_MODEL), lambda i: (i, 0)),
            pl.BlockSpec((tm, GDN_WIDTH), lambda i: (i, 0)),
            pl.BlockSpec((tm, SB_WIDTH), lambda i: (i, 0)),
            pl.BlockSpec((D_MODEL, D_MODEL), lambda i: (0, 0)),
            pl.BlockSpec((1, D_MODEL), lambda i: (0, 0)),
            pl.BlockSpec((LANES, D_MODEL), lambda i: (0, 0)),
            pl.BlockSpec((LANES, LANES), lambda i: (0, 0)),
        ],
        out_specs=[
            pl.BlockSpec((tm, D_MODEL), lambda i: (i, 0)),
            pl.BlockSpec((tm, ROW_TILES, LANES), lambda i: (i, 0, 0)),
            pl.BlockSpec((SUBLANES, tm), lambda i: (0, i)),
            pl.BlockSpec((tm, LANES), lambda i: (i, 0)),
            pl.BlockSpec((N_EXPERTS, LANES), lambda i: (0, 0)),
        ],
        out_shape=[
            jax.ShapeDtypeStruct((n_tok, D_MODEL), F32),
            jax.ShapeDtypeStruct((n_tok, ROW_TILES, LANES), F32),
            jax.ShapeDtypeStruct((SUBLANES, n_tok), F32),
            jax.ShapeDtypeStruct((n_tok, LANES), F32),
            jax.ShapeDtypeStruct((N_EXPERTS, LANES), F32),
        ],
        scratch_shapes=[pltpu.VMEM((N_EXPERTS, LANES), F32)],
        compiler_params=_cparams(1),
        name="post_mix",
    )(x2d, gdn_out, sb_out, w_out, ln, w_router_t, b_router)


def _dispatch_kernel(d1_ref, d2_ref, rows_ref, buf_in_hbm, buf_hbm, sem):
    del buf_in_hbm
    tm = rows_ref.shape[0]

    def issue(t, carry):
        pltpu.make_async_copy(rows_ref.at[t], buf_hbm.at[d1_ref[0, 0, t]], sem).start(priority=0)
        pltpu.make_async_copy(rows_ref.at[t], buf_hbm.at[d2_ref[0, 0, t]], sem).start(priority=1)
        return carry

    lax.fori_loop(0, tm, issue, 0)
    for _ in range(2):
        pltpu.make_async_copy(rows_ref, buf_hbm.at[pl.ds(0, tm)], sem).wait()


def _dispatch(dest1, dest2, h2r, n_rows):
    n_tok = h2r.shape[0]
    tm = min(DISP_TM, n_tok)
    n_steps = n_tok // tm
    d1 = dest1.reshape(n_steps, 1, tm)
    d2 = dest2.reshape(n_steps, 1, tm)
    smem_spec = pl.BlockSpec((1, 1, tm), lambda i: (i, 0, 0), memory_space=pltpu.SMEM)
    buf0 = jnp.zeros((n_rows, ROW_TILES, LANES), F32)
    return pl.pallas_call(
        _dispatch_kernel,
        grid=(n_steps,),
        in_specs=[smem_spec, smem_spec,
                  pl.BlockSpec((tm, ROW_TILES, LANES), lambda i: (i, 0, 0)),
                  pl.BlockSpec(memory_space=pl.ANY)],
        out_specs=pl.BlockSpec(memory_space=pl.ANY),
        out_shape=jax.ShapeDtypeStruct((n_rows, ROW_TILES, LANES), F32),
        scratch_shapes=[pltpu.SemaphoreType.DMA(())],
        input_output_aliases={3: 0},
        compiler_params=_cparams(1),
        name="moe_dispatch",
    )(d1, d2, h2r, buf0)


def _ffn_kernel(be_ref, x_ref, wg_ref, wu_ref, wd_ref, y_ref, wgu_bf, wd_bf):
    i = pl.program_id(0)

    @pl.when((i == 0) | (be_ref[i] != be_ref[jnp.maximum(i - 1, 0)]))
    def _():
        wgu_bf[:, :D_EXPERT] = wg_ref[0].astype(BF16)
        wgu_bf[:, D_EXPERT:] = wu_ref[0].astype(BF16)
        wd_bf[...] = wd_ref[0].astype(BF16)

    acc = jnp.dot(_tiles_to_rows(x_ref[...]).astype(BF16), wgu_bf[...], preferred_element_type=F32)
    gate, up = acc[:, :D_EXPERT], acc[:, D_EXPERT:]
    hidden = (gate * jax.nn.sigmoid(gate)) * up
    y_ref[...] = _rows_to_tiles(jnp.dot(hidden.astype(BF16), wd_bf[...], preferred_element_type=F32))


def _ffn(block_expert, buf, w_gate, w_up, w_down):
    n_rows = buf.shape[0]
    n_blocks = n_rows // MOE_BLOCK
    grid_spec = pltpu.PrefetchScalarGridSpec(
        num_scalar_prefetch=1,
        grid=(n_blocks,),
        in_specs=[
            pl.BlockSpec((MOE_BLOCK, ROW_TILES, LANES), lambda i, be: (i, 0, 0)),
            pl.BlockSpec((1, D_MODEL, D_EXPERT), lambda i, be: (be[i], 0, 0)),
            pl.BlockSpec((1, D_MODEL, D_EXPERT), lambda i, be: (be[i], 0, 0)),
            pl.BlockSpec((1, D_EXPERT, D_MODEL), lambda i, be: (be[i], 0, 0)),
        ],
        out_specs=pl.BlockSpec((MOE_BLOCK, ROW_TILES, LANES), lambda i, be: (i, 0, 0)),
        scratch_shapes=[pltpu.VMEM((D_MODEL, 2 * D_EXPERT), BF16),
                        pltpu.VMEM((D_EXPERT, D_MODEL), BF16)],
    )
    return pl.pallas_call(
        _ffn_kernel,
        grid_spec=grid_spec,
        out_shape=jax.ShapeDtypeStruct((n_rows, ROW_TILES, LANES), F32),
        compiler_params=_cparams(1),
        name="moe_ffn",
    )(block_expert, buf, w_gate, w_up, w_down)


def _final_kernel(d1_ref, d2_ref, ybuf_hbm, x1_ref, gcol_ref, p_ref, wpg_ref, wpp_ref,
                  lnp_ref, lnf_ref, out_ref, y1_ref, y2_ref, sem):
    tm = x1_ref.shape[0]

    def issue(t, carry):
        pltpu.make_async_copy(ybuf_hbm.at[d1_ref[0, 0, t]], y1_ref.at[t], sem).start(priority=0)
        pltpu.make_async_copy(ybuf_hbm.at[d2_ref[0, 0, t]], y2_ref.at[t], sem).start(priority=1)
        return carry

    lax.fori_loop(0, tm, issue, 0)
    pltpu.make_async_copy(ybuf_hbm.at[pl.ds(0, tm)], y1_ref, sem).wait()
    pltpu.make_async_copy(ybuf_hbm.at[pl.ds(0, tm)], y2_ref, sem).wait()

    g1 = gcol_ref[:, 0:1]
    g2 = gcol_ref[:, 1:2]
    x2 = x1_ref[...] + (g1 * _tiles_to_rows(y1_ref[...]) + g2 * _tiles_to_rows(y2_ref[...]))
    hp = _rms(x2, lnp_ref[...]).astype(BF16)
    ple_gate = jax.nn.sigmoid(jnp.dot(hp, wpg_ref[...], preferred_element_type=F32))
    ple = jnp.dot(p_ref[...].astype(BF16), wpp_ref[...], preferred_element_type=F32)
    x3 = x2 + ple_gate * ple
    out_ref[...] = _rms(x3, lnf_ref[...])


def _final(dest1, dest2, ybuf, x1, gcol, p2d, w_ple_gate, w_ple_proj, ln_ple, ln_final):
    n_tok = x1.shape[0]
    tm = min(FINAL_TM, n_tok)
    n_steps = n_tok // tm
    d1 = dest1.reshape(n_steps, 1, tm)
    d2 = dest2.reshape(n_steps, 1, tm)
    smem_spec = pl.BlockSpec((1, 1, tm), lambda i: (i, 0, 0), memory_space=pltpu.SMEM)
    return pl.pallas_call(
        _final_kernel,
        grid=(n_steps,),
        in_specs=[
            smem_spec, smem_spec,
            pl.BlockSpec(memory_space=pl.ANY),
            pl.BlockSpec((tm, D_MODEL), lambda i: (i, 0)),
            pl.BlockSpec((tm, LANES), lambda i: (i, 0)),
            pl.BlockSpec((tm, PLE_DIM), lambda i: (i, 0)),
            pl.BlockSpec((D_MODEL, D_MODEL), lambda i: (0, 0)),
            pl.BlockSpec((PLE_DIM, D_MODEL), lambda i: (0, 0)),
            pl.BlockSpec((1, D_MODEL), lambda i: (0, 0)),
            pl.BlockSpec((1, D_MODEL), lambda i: (0, 0)),
        ],
        out_specs=pl.BlockSpec((tm, D_MODEL), lambda i: (i, 0)),
        out_shape=jax.ShapeDtypeStruct((n_tok, D_MODEL), F32),
        scratch_shapes=[pltpu.VMEM((tm, ROW_TILES, LANES), F32),
                        pltpu.VMEM((tm, ROW_TILES, LANES), F32),
                        pltpu.SemaphoreType.DMA(())],
        compiler_params=_cparams(1),
        name="moe_combine_ple_final",
    )(d1, d2, ybuf, x1, gcol, p2d, w_ple_gate, w_ple_proj, ln_ple, ln_final)


def _layer(x2d, p2d, bsz, seq, ln_mix, w_in, conv_w, a_log, dt_bias, gdn_norm_w, sb_norm_w, w_out,
           ln_moe, w_rg, b_rg, w_re, b_re, w_eg, w_eu, w_ed, ln_ple, w_ple_gate, w_ple_proj, ln_final):
    n_tok = bsz * seq
    C, H = GDN_CHUNK, GDN_HEADS
    o_z = 3 * GDN_WIDTH
    o_b = o_z + GDN_WIDTH
    o_q = o_b + 2 * H
    w_all = jnp.concatenate(
        [w_in[:, :o_b], w_in[:, o_q:], w_in[:, o_b:o_q],
         jnp.zeros((D_MODEL, LANES - 2 * H), F32)], axis=1).astype(BF16)
    proj, ba = _in_proj(x2d, ln_mix.reshape(1, D_MODEL), w_all)
    proj3 = proj.reshape(bsz, seq, PROJ_W)
    ba_rows = (ba[:, :2 * H].reshape(bsz, seq // C, C, 2, H)
               .transpose(0, 1, 3, 4, 2).reshape(bsz, seq // C, 2, GDN_HC))
    gparams = jnp.repeat(jnp.stack([a_log, dt_bias]).astype(F32), C, axis=1)

    gdn_out = _gdn(proj3, ba_rows, conv_w, gparams, gdn_norm_w.reshape(1, GDN_HEAD_DIM))
    sb_out = _sb(proj3, jnp.tile(sb_norm_w, LANES // SB_HEAD_DIM).reshape(1, LANES))

    w_router_t = jnp.zeros((LANES, D_MODEL), F32)
    w_router_t = w_router_t.at[:N_EXPERTS].set(w_re.T).at[N_EXPERTS:N_EXPERTS + N_GROUPS].set(w_rg.T)
    b_router = jnp.zeros((LANES,), F32).at[:N_EXPERTS].set(b_re).at[N_EXPERTS:N_EXPERTS + N_GROUPS].set(b_rg)
    b_router = jnp.broadcast_to(b_router[:, None], (LANES, LANES))
    x1, h2r, route, gcol, cnt = _post_mix(
        x2d, gdn_out.reshape(n_tok, GDN_WIDTH), sb_out.reshape(n_tok, SB_WIDTH),
        w_out.astype(BF16), ln_moe.reshape(1, D_MODEL), w_router_t, b_router)

    counts = cnt[:, 0].astype(I32)
    padded = (counts + MOE_BLOCK - 1) // MOE_BLOCK * MOE_BLOCK
    padded_end = jnp.cumsum(padded)
    padded_start = padded_end - padded
    route_i = route[:4].astype(I32)
    dest1 = padded_start[route_i[0]] + route_i[2]
    dest2 = padded_start[route_i[1]] + route_i[3]
    n_blocks = (2 * n_tok + MOE_BLOCK - 1) // MOE_BLOCK + N_EXPERTS
    block_row0 = jnp.arange(n_blocks, dtype=I32) * MOE_BLOCK
    block_expert = jnp.minimum(
        jnp.sum((padded_end[:, None] <= block_row0[None, :]).astype(I32), axis=0), N_EXPERTS - 1)

    buf = _dispatch(dest1, dest2, h2r, n_blocks * MOE_BLOCK)
    ybuf = _ffn(block_expert, buf, w_eg, w_eu, w_ed)
    return _final(dest1, dest2, ybuf, x1, gcol, p2d, w_ple_gate.astype(BF16), w_ple_proj.astype(BF16),
                  ln_ple.reshape(1, D_MODEL), ln_final.reshape(1, D_MODEL))


def kernel(x, p, ln_mix, w_in, conv_w, a_log, dt_bias, gdn_norm_w, sb_norm_w, w_out, ln_moe,
           w_router_group, b_router_group, w_router_expert, b_router_expert, w_expert_gate,
           w_expert_up, w_expert_down, ln_ple, w_ple_gate, w_ple_proj, ln_final):
    bsz, seq, _ = x.shape
    depth = p.shape[0]
    assert depth == 1, "the final RMSNorm is fused into the layer's last kernel"
    x2d = x.reshape(bsz * seq, D_MODEL)
    out = _layer(x2d, p[0].reshape(bsz * seq, PLE_DIM), bsz, seq, ln_mix[0], w_in[0], conv_w[0],
                 a_log[0], dt_bias[0], gdn_norm_w[0], sb_norm_w[0], w_out[0], ln_moe[0],
                 w_router_group[0], b_router_group[0], w_router_expert[0], b_router_expert[0],
                 w_expert_gate[0], w_expert_up[0], w_expert_down[0], ln_ple[0], w_ple_gate[0],
                 w_ple_proj[0], ln_final)
    return out.reshape(bsz, seq, D_MODEL)
```

```python
import jax
import jax.numpy as jnp
from jax import lax
from jax.experimental import pallas as pl
from jax.experimental.pallas import tpu as pltpu

F32 = jnp.float32
BF16 = jnp.bfloat16
I32 = jnp.int32

D_MODEL = 1024
PLE_DIM = 256
GDN_HEADS = 4
GDN_HEAD_DIM = 128
GDN_WIDTH = GDN_HEADS * GDN_HEAD_DIM
GDN_CONV = 4
GDN_CHUNK = 64
SB_HEADS = 8
SB_HEAD_DIM = 64
SB_WIDTH = SB_HEADS * SB_HEAD_DIM
N_GROUPS = 8
EXPERTS_PER_GROUP = 8
N_EXPERTS = N_GROUPS * EXPERTS_PER_GROUP
D_EXPERT = D_MODEL // 4
MOE_BLOCK = 256
EPS = 1e-6

LANES = 128
SUBLANES = 8
ROW_TILES = D_MODEL // LANES
PROJ_W = 3 * GDN_WIDTH + GDN_WIDTH + 3 * SB_WIDTH
PROJ_CHUNK = 512
PREV_ROWS = 16
VMEM_LIMIT = 48 * 1024 * 1024

IN_TM = 512
GDN_CHUNKS_PER_STEP = 4
GDN_HC = GDN_HEADS * GDN_CHUNK
SB_TILE = 256
POST_TM = 512
DISP_TM = 256
FINAL_TM = 256


def _cparams(n_axes):
    return pltpu.CompilerParams(dimension_semantics=("arbitrary",) * n_axes,
                                vmem_limit_bytes=VMEM_LIMIT)


def _rms(x, w):
    return x * lax.rsqrt(jnp.mean(x * x, axis=-1, keepdims=True) + EPS) * w


def _mm(a, b):
    return jnp.dot(a.astype(BF16), b.astype(BF16), preferred_element_type=F32)


def _mm_nt(a, b):
    return lax.dot_general(a.astype(BF16), b.astype(BF16), (((1,), (1,)), ((), ())),
                           preferred_element_type=F32)


def _mm_tn(a, b):
    return lax.dot_general(a.astype(BF16), b.astype(BF16), (((0,), (0,)), ((), ())),
                           preferred_element_type=F32)


def _split(a):
    hi = a.astype(BF16)
    lo = (a - hi.astype(F32)).astype(BF16)
    return hi, lo


def _softplus(x):
    return jnp.maximum(x, 0.0) + jnp.log(1.0 + jnp.exp(-jnp.abs(x)))


def _tiles_to_rows(x3):
    xt = jnp.transpose(x3, (1, 0, 2))
    return jnp.concatenate([xt[j] for j in range(ROW_TILES)], axis=1)


def _rows_to_tiles(x2):
    xt = jnp.stack([x2[:, j * LANES:(j + 1) * LANES] for j in range(ROW_TILES)], axis=0)
    return jnp.transpose(xt, (1, 0, 2))


def _in_proj_kernel(x_ref, ln_ref, w_ref, proj_ref, ba_ref):
    h = _rms(x_ref[...], ln_ref[...]).astype(BF16)
    for j in range(0, PROJ_W, PROJ_CHUNK):
        proj_ref[:, j:j + PROJ_CHUNK] = jnp.dot(
            h, w_ref[:, j:j + PROJ_CHUNK], preferred_element_type=F32).astype(BF16)
    ba_ref[...] = jnp.dot(h, w_ref[:, PROJ_W:PROJ_W + LANES], preferred_element_type=F32)


def _in_proj(x2d, ln, w_all):
    n_tok = x2d.shape[0]
    tm = min(IN_TM, n_tok)
    return pl.pallas_call(
        _in_proj_kernel,
        grid=(n_tok // tm,),
        in_specs=[pl.BlockSpec((tm, D_MODEL), lambda i: (i, 0)),
                  pl.BlockSpec((1, D_MODEL), lambda i: (0, 0)),
                  pl.BlockSpec((D_MODEL, PROJ_W + LANES), lambda i: (0, 0))],
        out_specs=[pl.BlockSpec((tm, PROJ_W), lambda i: (i, 0)),
                   pl.BlockSpec((tm, LANES), lambda i: (i, 0))],
        out_shape=[jax.ShapeDtypeStruct((n_tok, PROJ_W), BF16),
                   jax.ShapeDtypeStruct((n_tok, LANES), F32)],
        compiler_params=_cparams(1),
        name="in_proj",
    )(x2d, ln, w_all)


def _gdn_kernel(cur_ref, prev_ref, z_ref, ba_ref, convw_ref, gp_ref, nw_ref, out_ref, state_ref):
    C, H, Dh, HC = GDN_CHUNK, GDN_HEADS, GDN_HEAD_DIM, GDN_HC
    rows_per_step = GDN_CHUNKS_PER_STEP * C
    step = pl.program_id(1)

    @pl.when(step == 0)
    def _():
        state_ref[...] = jnp.zeros_like(state_ref)

    cur = cur_ref[0].astype(F32)
    prev = jnp.where(step > 0, prev_ref[0].astype(F32), 0.0)
    stacked = jnp.concatenate([prev, cur], axis=0)
    w = convw_ref[...]
    y = w[GDN_CONV - 1:GDN_CONV, :] * cur
    for k in range(GDN_CONV - 1):
        shift = GDN_CONV - 1 - k
        y = y + w[k:k + 1, :] * pltpu.roll(stacked, shift, axis=0)[PREV_ROWS:, :]
    qkv = y * jax.nn.sigmoid(y)

    ri = lax.broadcasted_iota(I32, (HC, HC), 0)
    ci = lax.broadcasted_iota(I32, (HC, HC), 1)
    same_head = (ri // C) == (ci // C)
    incl = same_head & (ci <= ri)
    strict = same_head & (ci < ri)
    diag = ci == ri
    eye = jnp.where(diag, 1.0, 0.0).astype(F32)
    gp = gp_ref[...]

    def stack_heads(rows, base):
        return jnp.concatenate([rows[:, base + h * Dh:base + (h + 1) * Dh] for h in range(H)], axis=0)

    def scalars(n):
        ba = ba_ref[0, n]
        beta_row = jax.nn.sigmoid(ba[0:1, :])
        g_row = -jnp.exp(gp[0:1, :]) * _softplus(ba[1:2, :] + gp[1:2, :])
        gc_col = jnp.sum(jnp.where(incl, jnp.broadcast_to(g_row, (HC, HC)), 0.0), axis=1, keepdims=True)
        beta_col = jnp.sum(jnp.where(diag, jnp.broadcast_to(beta_row, (HC, HC)), 0.0), axis=1, keepdims=True)
        gc_row = jnp.sum(jnp.where(diag, jnp.broadcast_to(gc_col, (HC, HC)), 0.0), axis=0, keepdims=True)
        decay = jnp.where(incl, jnp.exp(jnp.where(incl, gc_col - gc_row, 0.0)), 0.0)
        return gc_col, beta_col, decay

    def normalized(n):
        rows = qkv[n * C:(n + 1) * C, :]
        q = stack_heads(rows, 0)
        k = stack_heads(rows, GDN_WIDTH)
        v = stack_heads(rows, 2 * GDN_WIDTH)
        q = q * lax.rsqrt(jnp.sum(q * q, axis=-1, keepdims=True) + EPS) * (Dh ** -0.5)
        k = k * lax.rsqrt(jnp.sum(k * k, axis=-1, keepdims=True) + EPS)
        return q, k, v

    chunk_ids = range(GDN_CHUNKS_PER_STEP)
    qkv_n = [normalized(n) for n in chunk_ids]
    sc = [scalars(n) for n in chunk_ids]
    kk = [_mm_nt(k, k) for _, k, _ in qkv_n]
    p = [jnp.where(strict, -(beta_col * kk_n * decay), 0.0) for (_, beta_col, decay), kk_n in zip(sc, kk)]
    t_inv = [eye + p_n for p_n in p]
    for _ in range(5):
        p = [_mm(p_n, p_n) for p_n in p]
        t_inv = [t_n + _mm(t_n, p_n) for t_n, p_n in zip(t_inv, p)]
    exp_gc = [jnp.exp(gc_col) for gc_col, _, _ in sc]
    sol = [_mm(t_n, jnp.concatenate([v * beta_col, k * (beta_col * e_n)], axis=1))
           for t_n, (_, k, v), (_, beta_col, _), e_n in zip(t_inv, qkv_n, sc, exp_gc)]
    qk = [_mm_nt(q, k) * decay for (q, k, _), (_, _, decay) in zip(qkv_n, sc)]
    pre = [dict(u=sol[n][:, :Dh], wk=sol[n][:, Dh:], qk=qk[n], qe=qkv_n[n][0] * exp_gc[n],
                k=qkv_n[n][1], gc=sc[n][0]) for n in chunk_ids]

    for n in chunk_ids:
        c = pre[n]
        v_new, o_state = [], []
        for h in range(H):
            rows = slice(h * C, (h + 1) * C)
            ws = _mm(jnp.concatenate([c["wk"][rows], c["qe"][rows]], axis=0), state_ref[h])
            v_new.append(c["u"][rows] - ws[:C])
            o_state.append(ws[C:])
        v_new = jnp.concatenate(v_new, axis=0)
        o = jnp.concatenate(o_state, axis=0) + _mm(c["qk"], v_new)
        for h in range(H):
            rows = slice(h * C, (h + 1) * C)
            g_last = c["gc"][(h + 1) * C - 1:(h + 1) * C, :]
            k_dec = c["k"][rows] * jnp.exp(g_last - c["gc"][rows])
            state_ref[h] = state_ref[h] * jnp.exp(g_last) + _mm_tn(k_dec, v_new[rows])
        o_n = _rms(o, nw_ref[...])
        for h in range(H):
            zz = z_ref[0, n * C:(n + 1) * C, h * Dh:(h + 1) * Dh].astype(F32)
            out_ref[0, n * C:(n + 1) * C, h * Dh:(h + 1) * Dh] = (
                o_n[h * C:(h + 1) * C] * (zz * jax.nn.sigmoid(zz))).astype(BF16)


def _gdn(proj3, ba_rows, conv_w, gparams, norm_w):
    bsz, seq, _ = proj3.shape
    cps = GDN_CHUNKS_PER_STEP
    rows = cps * GDN_CHUNK
    prev_per_step = rows // PREV_ROWS
    return pl.pallas_call(
        _gdn_kernel,
        grid=(bsz, seq // rows),
        in_specs=[
            pl.BlockSpec((1, rows, 3 * GDN_WIDTH), lambda b, s: (b, s, 0)),
            pl.BlockSpec((1, PREV_ROWS, 3 * GDN_WIDTH),
                         lambda b, s: (b, jnp.maximum(s * prev_per_step - 1, 0), 0)),
            pl.BlockSpec((1, rows, GDN_WIDTH), lambda b, s: (b, s, 3)),
            pl.BlockSpec((1, cps, 2, GDN_HC), lambda b, s: (b, s, 0, 0)),
            pl.BlockSpec((GDN_CONV, 3 * GDN_WIDTH), lambda b, s: (0, 0)),
            pl.BlockSpec((2, GDN_HC), lambda b, s: (0, 0)),
            pl.BlockSpec((1, GDN_HEAD_DIM), lambda b, s: (0, 0)),
        ],
        out_specs=pl.BlockSpec((1, rows, GDN_WIDTH), lambda b, s: (b, s, 0)),
        out_shape=jax.ShapeDtypeStruct((bsz, seq, GDN_WIDTH), BF16),
        scratch_shapes=[pltpu.VMEM((GDN_HEADS, GDN_HEAD_DIM, GDN_HEAD_DIM), F32)],
        compiler_params=_cparams(2),
        name="gdn",
    )(proj3, proj3, proj3, ba_rows, conv_w, gparams, norm_w)


def _sb_kernel(q_ref, k_ref, v_ref, nw_ref, out_ref, acc_ref, later_ref):
    T, Dh = SB_TILE, SB_HEAD_DIM
    qb = pl.program_id(2)
    row = lax.broadcasted_iota(I32, (T, T), 0)
    col = lax.broadcasted_iota(I32, (T, T), 1)
    causal = col < row
    neg_suffix = jnp.where(row >= col, -1.0, 0.0).astype(BF16)
    neg_suffix2 = jnp.concatenate([neg_suffix, neg_suffix], axis=0)

    low = lax.broadcasted_iota(I32, (T, LANES), 1) < Dh
    q = q_ref[0] * jnp.asarray(Dh ** -0.5, BF16)
    zeros = jnp.zeros_like(q)
    q_heads = (jnp.where(low, q, zeros), jnp.where(low, zeros, q))

    def tiles(kbs, on_diagonal):
        blocks = []
        for kb in kbs:
            start = pl.multiple_of(kb * T, T)
            blocks.append((k_ref[0, pl.ds(start, T), :], v_ref[0, pl.ds(start, T), :]))
        chains = [(i, k_blk, v_blk) for i in range(2) for k_blk, v_blk in blocks]
        zs = [lax.dot_general(q_heads[i], k_blk, (((1,), (1,)), ((), ())), preferred_element_type=F32)
              for i, k_blk, _ in chains]
        splits = []
        for z in zs:
            neg_log_remain = _softplus(z)
            if on_diagonal:
                neg_log_remain = jnp.where(causal, neg_log_remain, 0.0)
            splits.append(neg_log_remain.astype(BF16))
        incl_after = [jnp.dot(s, neg_suffix, preferred_element_type=F32) for s in splits]
        later = [later_ref[0], later_ref[1]]
        weights = []
        for (i, _, _), z, incl in zip(chains, zs, incl_after):
            w = jnp.exp(z + incl + later[i])
            if on_diagonal:
                w = jnp.where(causal, w, 0.0)
            weights.append(w.astype(BF16))
            later[i] = later[i] + incl[:, 0:1]
        acc = [acc_ref[0], acc_ref[1]]
        for (i, _, v_blk), w in zip(chains, weights):
            acc[i] = acc[i] + jnp.dot(w, v_blk, preferred_element_type=F32)
        for i in range(2):
            later_ref[i] = later[i]
            acc_ref[i] = acc[i]

    later_ref[...] = jnp.zeros_like(later_ref)
    acc_ref[...] = jnp.zeros_like(acc_ref)
    tiles([qb], True)

    def quad(i, carry):
        kb = qb - 1 - 4 * i
        tiles([kb, kb - 1, kb - 2, kb - 3], False)
        return carry

    lax.fori_loop(0, qb // 4, quad, 0)
    left = qb % 4

    @pl.when(left >= 2)
    def _():
        tiles([left - 1, left - 2], False)

    @pl.when(left % 2 == 1)
    def _():
        tiles([0], False)

    o = jnp.where(low, acc_ref[0], acc_ref[1])
    hr = lax.broadcasted_iota(I32, (LANES, LANES), 0) // Dh
    hc = lax.broadcasted_iota(I32, (LANES, LANES), 1) // Dh
    avg = jnp.where(hr == hc, 1.0 / Dh, 0.0).astype(BF16)
    hi, lo = _split(o * o)
    ms = jnp.dot(jnp.concatenate([hi, lo], axis=1), jnp.concatenate([avg, avg], axis=0),
                 preferred_element_type=F32)
    out_ref[0] = (o * lax.rsqrt(ms + EPS) * nw_ref[...]).astype(BF16)


def _sb(proj3, norm_w2):
    bsz, seq, _ = proj3.shape
    tile = min(SB_TILE, seq)
    assert tile == SB_TILE and seq % tile == 0
    pairs = SB_WIDTH // LANES
    q_col = (3 * GDN_WIDTH + GDN_WIDTH) // LANES
    k_col = q_col + pairs
    v_col = k_col + pairs
    return pl.pallas_call(
        _sb_kernel,
        grid=(bsz, pairs, seq // tile),
        in_specs=[
            pl.BlockSpec((1, tile, LANES), lambda b, hp, qb: (b, qb, q_col + hp)),
            pl.BlockSpec((1, seq, LANES), lambda b, hp, qb: (b, 0, k_col + hp)),
            pl.BlockSpec((1, seq, LANES), lambda b, hp, qb: (b, 0, v_col + hp)),
            pl.BlockSpec((1, LANES), lambda b, hp, qb: (0, 0)),
        ],
        out_specs=pl.BlockSpec((1, tile, LANES), lambda b, hp, qb: (b, qb, hp)),
        out_shape=jax.ShapeDtypeStruct((bsz, seq, SB_WIDTH), BF16),
        scratch_shapes=[pltpu.VMEM((2, tile, LANES), F32),
                        pltpu.VMEM((2, tile, 1), F32)],
        compiler_params=_cparams(3),
        name="sb_attn",
    )(proj3, proj3, proj3, norm_w2)


def _post_mix_kernel(x_ref, g_ref, s_ref, wo_ref, ln_ref, wrt_ref, br_ref,
                     x1_ref, h2r_ref, route_ref, gcol_ref, cnt_ref, count_ref):
    tm = x_ref.shape[0]
    i = pl.program_id(0)

    @pl.when(i == 0)
    def _():
        count_ref[...] = jnp.zeros_like(count_ref)

    x1 = (x_ref[...]
          + jnp.dot(g_ref[...], wo_ref[0:GDN_WIDTH, :], preferred_element_type=F32)
          + jnp.dot(s_ref[...], wo_ref[GDN_WIDTH:, :], preferred_element_type=F32))
    x1_ref[...] = x1
    h2 = _rms(x1, ln_ref[...])
    h2r_ref[...] = _rows_to_tiles(h2)

    w_hi, w_lo = _split(wrt_ref[...])
    h_hi, h_lo = _split(h2)
    nt = (((1,), (1,)), ((), ()))
    logits = (lax.dot_general(w_hi, h_hi, nt, preferred_element_type=F32)
              + lax.dot_general(w_hi, h_lo, nt, preferred_element_type=F32)
              + lax.dot_general(w_lo, h_hi, nt, preferred_element_type=F32)) + br_ref[:, 0:1]

    g_logits = logits[N_EXPERTS:N_EXPERTS + N_GROUPS, :]
    g_max = jnp.max(g_logits, axis=0, keepdims=True)
    group_w = 1.0 / jnp.sum(jnp.exp(g_logits - g_max), axis=0, keepdims=True)
    sub8 = lax.broadcasted_iota(I32, (N_GROUPS, tm), 0)
    g_idx = jnp.min(jnp.where(g_logits == g_max, sub8, N_GROUPS), axis=0, keepdims=True)

    e_logits = logits[0:N_EXPERTS, :]
    erow = lax.broadcasted_iota(I32, (N_EXPERTS, tm), 0)
    neg_inf = -jnp.inf
    in_group = jnp.where((erow // EXPERTS_PER_GROUP) == g_idx, e_logits, neg_inf)
    m1 = jnp.max(in_group, axis=0, keepdims=True)
    idx1 = jnp.min(jnp.where(in_group == m1, erow, N_EXPERTS), axis=0, keepdims=True)
    rest = jnp.where(erow == idx1, neg_inf, in_group)
    m2 = jnp.max(rest, axis=0, keepdims=True)
    idx2 = jnp.min(jnp.where(rest == m2, erow, N_EXPERTS), axis=0, keepdims=True)
    r = jnp.exp(m2 - m1)
    w1 = 1.0 / (1.0 + r)
    gate1 = group_w * w1
    gate2 = group_w * (r * w1)

    oh1 = erow == idx1
    oh2 = erow == idx2
    picked = jnp.where(oh1 | oh2, 1.0, 0.0)
    tr = lax.broadcasted_iota(I32, (tm, tm), 0)
    tc = lax.broadcasted_iota(I32, (tm, tm), 1)
    before = jnp.where(tr < tc, 1.0, 0.0).astype(BF16)
    seen = jnp.dot(picked.astype(BF16), before, preferred_element_type=F32) + count_ref[:, 0:1]
    rank1 = jnp.sum(jnp.where(oh1, seen, 0.0), axis=0, keepdims=True)
    rank2 = jnp.sum(jnp.where(oh2, seen, 0.0), axis=0, keepdims=True)
    count_ref[...] = count_ref[...] + jnp.sum(picked, axis=1, keepdims=True)
    cnt_ref[...] = count_ref[...]

    sub = lax.broadcasted_iota(I32, (SUBLANES, tm), 0)
    rows = (idx1.astype(F32), idx2.astype(F32), rank1, rank2)
    route = jnp.zeros((SUBLANES, tm), F32)
    for n, val in enumerate(rows):
        route = jnp.where(sub == n, val, route)
    route_ref[...] = route

    sub_l = lax.broadcasted_iota(I32, (LANES, tm), 0)
    gates_t = jnp.where(sub_l == 0, gate1, jnp.where(sub_l == 1, gate2, 0.0))
    gcol_ref[...] = gates_t.T


def _post_mix(x2d, gdn_out, sb_out, w_out, ln, w_router_t, b_router):
    n_tok = x2d.shape[0]
    tm = min(POST_TM, n_tok)
    return pl.pallas_call(
        _post_mix_kernel,
        grid=(n_tok // tm,),
        in_specs=[
            pl.BlockSpec((tm, D_MODEL), lambda i: (i, 0)),
            pl.BlockSpec((tm, GDN_WIDTH), lambda i: (i, 0)),
            pl.BlockSpec((tm, SB_WIDTH), lambda i: (i, 0)),
            pl.BlockSpec((D_MODEL, D_MODEL), lambda i: (0, 0)),
            pl.BlockSpec((1, D_MODEL), lambda i: (0, 0)),
            pl.BlockSpec((LANES, D_MODEL), lambda i: (0, 0)),
            pl.BlockSpec((LANES, LANES), lambda i: (0, 0)),
        ],
        out_specs=[
            pl.BlockSpec((tm, D_MODEL), lambda i: (i, 0)),
            pl.BlockSpec((tm, ROW_TILES, LANES), lambda i: (i, 0, 0)),
            pl.BlockSpec((SUBLANES, tm), lambda i: (0, i)),
            pl.BlockSpec((tm, LANES), lambda i: (i, 0)),
            pl.BlockSpec((N_EXPERTS, LANES), lambda i: (0, 0)),
        ],
        out_shape=[
            jax.ShapeDtypeStruct((n_tok, D_MODEL), F32),
            jax.ShapeDtypeStruct((n_tok, ROW_TILES, LANES), F32),
            jax.ShapeDtypeStruct((SUBLANES, n_tok), F32),
            jax.ShapeDtypeStruct((n_tok, LANES), F32),
            jax.ShapeDtypeStruct((N_EXPERTS, LANES), F32),
        ],
        scratch_shapes=[pltpu.VMEM((N_EXPERTS, LANES), F32)],
        compiler_params=_cparams(1),
        name="post_mix",
    )(x2d, gdn_out, sb_out, w_out, ln, w_router_t, b_router)


def _dispatch_kernel(d1_ref, d2_ref, rows_ref, buf_in_hbm, buf_hbm, sem):
    del buf_in_hbm
    tm = rows_ref.shape[0]

    def issue(t, carry):
        pltpu.make_async_copy(rows_ref.at[t], buf_hbm.at[d1_ref[0, 0, t]], sem).start(priority=0)
        pltpu.make_async_copy(rows_ref.at[t], buf_hbm.at[d2_ref[0, 0, t]], sem).start(priority=1)
        return carry

    lax.fori_loop(0, tm, issue, 0)
    for _ in range(2):
        pltpu.make_async_copy(rows_ref, buf_hbm.at[pl.ds(0, tm)], sem).wait()


def _dispatch(dest1, dest2, h2r, n_rows):
    n_tok = h2r.shape[0]
    tm = min(DISP_TM, n_tok)
    n_steps = n_tok // tm
    d1 = dest1.reshape(n_steps, 1, tm)
    d2 = dest2.reshape(n_steps, 1, tm)
    smem_spec = pl.BlockSpec((1, 1, tm), lambda i: (i, 0, 0), memory_space=pltpu.SMEM)
    buf0 = jnp.zeros((n_rows, ROW_TILES, LANES), F32)
    return pl.pallas_call(
        _dispatch_kernel,
        grid=(n_steps,),
        in_specs=[smem_spec, smem_spec,
                  pl.BlockSpec((tm, ROW_TILES, LANES), lambda i: (i, 0, 0)),
                  pl.BlockSpec(memory_space=pl.ANY)],
        out_specs=pl.BlockSpec(memory_space=pl.ANY),
        out_shape=jax.ShapeDtypeStruct((n_rows, ROW_TILES, LANES), F32),
        scratch_shapes=[pltpu.SemaphoreType.DMA(())],
        input_output_aliases={3: 0},
        compiler_params=_cparams(1),
        name="moe_dispatch",
    )(d1, d2, h2r, buf0)


def _ffn_kernel(be_ref, x_ref, wg_ref, wu_ref, wd_ref, y_ref, wgu_bf, wd_bf):
    i = pl.program_id(0)

    @pl.when((i == 0) | (be_ref[i] != be_ref[jnp.maximum(i - 1, 0)]))
    def _():
        wgu_bf[:, :D_EXPERT] = wg_ref[0].astype(BF16)
        wgu_bf[:, D_EXPERT:] = wu_ref[0].astype(BF16)
        wd_bf[...] = wd_ref[0].astype(BF16)

    acc = jnp.dot(_tiles_to_rows(x_ref[...]).astype(BF16), wgu_bf[...], preferred_element_type=F32)
    gate, up = acc[:, :D_EXPERT], acc[:, D_EXPERT:]
    hidden = (gate * jax.nn.sigmoid(gate)) * up
    y_ref[...] = _rows_to_tiles(jnp.dot(hidden.astype(BF16), wd_bf[...], preferred_element_type=F32))


def _ffn(block_expert, buf, w_gate, w_up, w_down):
    n_rows = buf.shape[0]
    n_blocks = n_rows // MOE_BLOCK
    grid_spec = pltpu.PrefetchScalarGridSpec(
        num_scalar_prefetch=1,
        grid=(n_blocks,),
        in_specs=[
            pl.BlockSpec((MOE_BLOCK, ROW_TILES, LANES), lambda i, be: (i, 0, 0)),
            pl.BlockSpec((1, D_MODEL, D_EXPERT), lambda i, be: (be[i], 0, 0)),
            pl.BlockSpec((1, D_MODEL, D_EXPERT), lambda i, be: (be[i], 0, 0)),
            pl.BlockSpec((1, D_EXPERT, D_MODEL), lambda i, be: (be[i], 0, 0)),
        ],
        out_specs=pl.BlockSpec((MOE_BLOCK, ROW_TILES, LANES), lambda i, be: (i, 0, 0)),
        scratch_shapes=[pltpu.VMEM((D_MODEL, 2 * D_EXPERT), BF16),
                        pltpu.VMEM((D_EXPERT, D_MODEL), BF16)],
    )
    return pl.pallas_call(
        _ffn_kernel,
        grid_spec=grid_spec,
        out_shape=jax.ShapeDtypeStruct((n_rows, ROW_TILES, LANES), F32),
        compiler_params=_cparams(1),
        name="moe_ffn",
    )(block_expert, buf, w_gate, w_up, w_down)


def _final_kernel(d1_ref, d2_ref, d1_next_ref, d2_next_ref, ybuf_hbm, x1_ref, gcol_ref, p_ref, wpg_ref,
                  wpp_ref, lnp_ref, lnf_ref, out_ref, y1_ref, y2_ref, sem):
    tm = x1_ref.shape[0]
    i = pl.program_id(0)
    slot = i % 2

    def gather(idx1_ref, idx2_ref, dst_slot):
        def issue(t, carry):
            pltpu.make_async_copy(ybuf_hbm.at[idx1_ref[0, 0, t]], y1_ref.at[dst_slot, t],
                                  sem.at[dst_slot]).start(priority=0)
            pltpu.make_async_copy(ybuf_hbm.at[idx2_ref[0, 0, t]], y2_ref.at[dst_slot, t],
                                  sem.at[dst_slot]).start(priority=1)
            return carry
        lax.fori_loop(0, tm, issue, 0)

    @pl.when(i == 0)
    def _():
        gather(d1_ref, d2_ref, 0)

    @pl.when(i + 1 < pl.num_programs(0))
    def _():
        gather(d1_next_ref, d2_next_ref, 1 - slot)

    pltpu.make_async_copy(ybuf_hbm.at[pl.ds(0, tm)], y1_ref.at[slot], sem.at[slot]).wait()
    pltpu.make_async_copy(ybuf_hbm.at[pl.ds(0, tm)], y2_ref.at[slot], sem.at[slot]).wait()

    g1 = gcol_ref[:, 0:1]
    g2 = gcol_ref[:, 1:2]
    x2 = x1_ref[...] + (g1 * _tiles_to_rows(y1_ref[slot]) + g2 * _tiles_to_rows(y2_ref[slot]))
    hp = _rms(x2, lnp_ref[...]).astype(BF16)
    ple_gate = jax.nn.sigmoid(jnp.dot(hp, wpg_ref[...], preferred_element_type=F32))
    ple = jnp.dot(p_ref[...].astype(BF16), wpp_ref[...], preferred_element_type=F32)
    x3 = x2 + ple_gate * ple
    out_ref[...] = _rms(x3, lnf_ref[...])


def _final(dest1, dest2, ybuf, x1, gcol, p2d, w_ple_gate, w_ple_proj, ln_ple, ln_final):
    n_tok = x1.shape[0]
    tm = min(FINAL_TM, n_tok)
    n_steps = n_tok // tm
    d1 = dest1.reshape(n_steps, 1, tm)
    d2 = dest2.reshape(n_steps, 1, tm)
    smem_spec = pl.BlockSpec((1, 1, tm), lambda i: (i, 0, 0), memory_space=pltpu.SMEM)
    smem_next = pl.BlockSpec((1, 1, tm), lambda i: (jnp.minimum(i + 1, n_steps - 1), 0, 0),
                             memory_space=pltpu.SMEM)
    return pl.pallas_call(
        _final_kernel,
        grid=(n_steps,),
        in_specs=[
            smem_spec, smem_spec, smem_next, smem_next,
            pl.BlockSpec(memory_space=pl.ANY),
            pl.BlockSpec((tm, D_MODEL), lambda i: (i, 0)),
            pl.BlockSpec((tm, LANES), lambda i: (i, 0)),
            pl.BlockSpec((tm, PLE_DIM), lambda i: (i, 0)),
            pl.BlockSpec((D_MODEL, D_MODEL), lambda i: (0, 0)),
            pl.BlockSpec((PLE_DIM, D_MODEL), lambda i: (0, 0)),
            pl.BlockSpec((1, D_MODEL), lambda i: (0, 0)),
            pl.BlockSpec((1, D_MODEL), lambda i: (0, 0)),
        ],
        out_specs=pl.BlockSpec((tm, D_MODEL), lambda i: (i, 0)),
        out_shape=jax.ShapeDtypeStruct((n_tok, D_MODEL), F32),
        scratch_shapes=[pltpu.VMEM((2, tm, ROW_TILES, LANES), F32),
                        pltpu.VMEM((2, tm, ROW_TILES, LANES), F32),
                        pltpu.SemaphoreType.DMA((2,))],
        compiler_params=_cparams(1),
        name="moe_combine_ple_final",
    )(d1, d2, d1, d2, ybuf, x1, gcol, p2d, w_ple_gate, w_ple_proj, ln_ple, ln_final)


def _layer(x2d, p2d, bsz, seq, ln_mix, w_in, conv_w, a_log, dt_bias, gdn_norm_w, sb_norm_w, w_out,
           ln_moe, w_rg, b_rg, w_re, b_re, w_eg, w_eu, w_ed, ln_ple, w_ple_gate, w_ple_proj, ln_final):
    n_tok = bsz * seq
    C, H = GDN_CHUNK, GDN_HEADS
    o_z = 3 * GDN_WIDTH
    o_b = o_z + GDN_WIDTH
    o_q = o_b + 2 * H
    w_all = jnp.concatenate(
        [w_in[:, :o_b], w_in[:, o_q:], w_in[:, o_b:o_q],
         jnp.zeros((D_MODEL, LANES - 2 * H), F32)], axis=1).astype(BF16)
    proj, ba = _in_proj(x2d, ln_mix.reshape(1, D_MODEL), w_all)
    proj3 = proj.reshape(bsz, seq, PROJ_W)
    ba_rows = (ba[:, :2 * H].reshape(bsz, seq // C, C, 2, H)
               .transpose(0, 1, 3, 4, 2).reshape(bsz, seq // C, 2, GDN_HC))
    gparams = jnp.repeat(jnp.stack([a_log, dt_bias]).astype(F32), C, axis=1)

    gdn_out = _gdn(proj3, ba_rows, conv_w, gparams, gdn_norm_w.reshape(1, GDN_HEAD_DIM))
    sb_out = _sb(proj3, jnp.tile(sb_norm_w, LANES // SB_HEAD_DIM).reshape(1, LANES))

    w_router_t = jnp.zeros((LANES, D_MODEL), F32)
    w_router_t = w_router_t.at[:N_EXPERTS].set(w_re.T).at[N_EXPERTS:N_EXPERTS + N_GROUPS].set(w_rg.T)
    b_router = jnp.zeros((LANES,), F32).at[:N_EXPERTS].set(b_re).at[N_EXPERTS:N_EXPERTS + N_GROUPS].set(b_rg)
    b_router = jnp.broadcast_to(b_router[:, None], (LANES, LANES))
    x1, h2r, route, gcol, cnt = _post_mix(
        x2d, gdn_out.reshape(n_tok, GDN_WIDTH), sb_out.reshape(n_tok, SB_WIDTH),
        w_out.astype(BF16), ln_moe.reshape(1, D_MODEL), w_router_t, b_router)

    counts = cnt[:, 0].astype(I32)
    padded = (counts + MOE_BLOCK - 1) // MOE_BLOCK * MOE_BLOCK
    padded_end = jnp.cumsum(padded)
    padded_start = padded_end - padded
    route_i = route[:4].astype(I32)
    dest1 = padded_start[route_i[0]] + route_i[2]
    dest2 = padded_start[route_i[1]] + route_i[3]
    n_blocks = (2 * n_tok + MOE_BLOCK - 1) // MOE_BLOCK + N_EXPERTS
    block_row0 = jnp.arange(n_blocks, dtype=I32) * MOE_BLOCK
    block_expert = jnp.minimum(
        jnp.sum((padded_end[:, None] <= block_row0[None, :]).astype(I32), axis=0), N_EXPERTS - 1)

    buf = _dispatch(dest1, dest2, h2r, n_blocks * MOE_BLOCK)
    ybuf = _ffn(block_expert, buf, w_eg, w_eu, w_ed)
    return _final(dest1, dest2, ybuf, x1, gcol, p2d, w_ple_gate.astype(BF16), w_ple_proj.astype(BF16),
                  ln_ple.reshape(1, D_MODEL), ln_final.reshape(1, D_MODEL))


def kernel(x, p, ln_mix, w_in, conv_w, a_log, dt_bias, gdn_norm_w, sb_norm_w, w_out, ln_moe,
           w_router_group, b_router_group, w_router_expert, b_router_expert, w_expert_gate,
           w_expert_up, w_expert_down, ln_ple, w_ple_gate, w_ple_proj, ln_final):
    bsz, seq, _ = x.shape
    depth = p.shape[0]
    assert depth == 1, "the final RMSNorm is fused into the layer's last kernel"
    x2d = x.reshape(bsz * seq, D_MODEL)
    out = _layer(x2d, p[0].reshape(bsz * seq, PLE_DIM), bsz, seq, ln_mix[0], w_in[0], conv_w[0],
                 a_log[0], dt_bias[0], gdn_norm_w[0], sb_norm_w[0], w_out[0], ln_moe[0],
                 w_router_group[0], b_router_group[0], w_router_expert[0], b_router_expert[0],
                 w_expert_gate[0], w_expert_up[0], w_expert_down[0], ln_ple[0], w_ple_gate[0],
                 w_ple_proj[0], ln_final)
    return out.reshape(bsz, seq, D_MODEL)
```

```python
import jax
import jax.numpy as jnp
from jax import lax
from jax.experimental import pallas as pl
from jax.experimental.pallas import tpu as pltpu

F32 = jnp.float32
BF16 = jnp.bfloat16
I32 = jnp.int32

D_MODEL = 1024
PLE_DIM = 256
GDN_HEADS = 4
GDN_HEAD_DIM = 128
GDN_WIDTH = GDN_HEADS * GDN_HEAD_DIM
GDN_CONV = 4
GDN_CHUNK = 64
SB_HEADS = 8
SB_HEAD_DIM = 64
SB_WIDTH = SB_HEADS * SB_HEAD_DIM
N_GROUPS = 8
EXPERTS_PER_GROUP = 8
N_EXPERTS = N_GROUPS * EXPERTS_PER_GROUP
D_EXPERT = D_MODEL // 4
MOE_BLOCK = 256
EPS = 1e-6

LANES = 128
SUBLANES = 8
ROW_TILES = D_MODEL // LANES
PROJ_W = 3 * GDN_WIDTH + GDN_WIDTH + 3 * SB_WIDTH
PROJ_CHUNK = 512
PREV_ROWS = 16
VMEM_LIMIT = 48 * 1024 * 1024

IN_TM = 512
GDN_CHUNKS_PER_STEP = 4
GDN_HC = GDN_HEADS * GDN_CHUNK
SB_TILE = 256
POST_TM = 512
DISP_TM = 256
FINAL_TM = 512
DMA_ISSUE_UNROLL = 8


def _cparams(n_axes):
    return pltpu.CompilerParams(dimension_semantics=("arbitrary",) * n_axes,
                                vmem_limit_bytes=VMEM_LIMIT)


def _rms(x, w):
    return x * lax.rsqrt(jnp.mean(x * x, axis=-1, keepdims=True) + EPS) * w


def _mm(a, b):
    return jnp.dot(a.astype(BF16), b.astype(BF16), preferred_element_type=F32)


def _mm_nt(a, b):
    return lax.dot_general(a.astype(BF16), b.astype(BF16), (((1,), (1,)), ((), ())),
                           preferred_element_type=F32)


def _mm_tn(a, b):
    return lax.dot_general(a.astype(BF16), b.astype(BF16), (((0,), (0,)), ((), ())),
                           preferred_element_type=F32)


def _split(a):
    hi = a.astype(BF16)
    lo = (a - hi.astype(F32)).astype(BF16)
    return hi, lo


LOG2_E = 1.4426950408889634


def _softplus(x):
    return jnp.maximum(x, 0.0) + jnp.log(1.0 + jnp.exp2(jnp.abs(x) * (-LOG2_E)))


def _tiles_to_rows(x3):
    xt = jnp.transpose(x3, (1, 0, 2))
    return jnp.concatenate([xt[j] for j in range(ROW_TILES)], axis=1)


def _rows_to_tiles(x2):
    xt = jnp.stack([x2[:, j * LANES:(j + 1) * LANES] for j in range(ROW_TILES)], axis=0)
    return jnp.transpose(xt, (1, 0, 2))


def _in_proj_kernel(x_ref, ln_ref, w_ref, proj_ref, ba_ref):
    h = _rms(x_ref[...], ln_ref[...]).astype(BF16)
    for j in range(0, PROJ_W, PROJ_CHUNK):
        proj_ref[:, j:j + PROJ_CHUNK] = jnp.dot(
            h, w_ref[:, j:j + PROJ_CHUNK], preferred_element_type=F32).astype(BF16)
    ba_ref[...] = jnp.dot(h, w_ref[:, PROJ_W:PROJ_W + LANES], preferred_element_type=F32)


def _in_proj(x2d, ln, w_all):
    n_tok = x2d.shape[0]
    tm = min(IN_TM, n_tok)
    return pl.pallas_call(
        _in_proj_kernel,
        grid=(n_tok // tm,),
        in_specs=[pl.BlockSpec((tm, D_MODEL), lambda i: (i, 0)),
                  pl.BlockSpec((1, D_MODEL), lambda i: (0, 0)),
                  pl.BlockSpec((D_MODEL, PROJ_W + LANES), lambda i: (0, 0),
                               pipeline_mode=pl.Buffered(1))],
        out_specs=[pl.BlockSpec((tm, PROJ_W), lambda i: (i, 0)),
                   pl.BlockSpec((tm, LANES), lambda i: (i, 0))],
        out_shape=[jax.ShapeDtypeStruct((n_tok, PROJ_W), BF16),
                   jax.ShapeDtypeStruct((n_tok, LANES), F32)],
        compiler_params=_cparams(1),
        name="in_proj",
    )(x2d, ln, w_all)


def _gdn_kernel(cur_ref, prev_ref, z_ref, ba_ref, convw_ref, gp_ref, nw_ref, out_ref, state_ref):
    C, H, Dh, HC = GDN_CHUNK, GDN_HEADS, GDN_HEAD_DIM, GDN_HC
    rows_per_step = GDN_CHUNKS_PER_STEP * C
    step = pl.program_id(1)

    @pl.when(step == 0)
    def _():
        state_ref[...] = jnp.zeros_like(state_ref)

    cur = cur_ref[0].astype(F32)
    prev = jnp.where(step > 0, prev_ref[0].astype(F32), 0.0)
    stacked = jnp.concatenate([prev, cur], axis=0)
    w = convw_ref[...]
    y = w[GDN_CONV - 1:GDN_CONV, :] * cur
    for k in range(GDN_CONV - 1):
        shift = GDN_CONV - 1 - k
        y = y + w[k:k + 1, :] * pltpu.roll(stacked, shift, axis=0)[PREV_ROWS:, :]
    qkv = y * jax.nn.sigmoid(y)

    ri = lax.broadcasted_iota(I32, (HC, HC), 0)
    ci = lax.broadcasted_iota(I32, (HC, HC), 1)
    same_head = (ri // C) == (ci // C)
    incl = same_head & (ci <= ri)
    strict = same_head & (ci < ri)
    diag = ci == ri
    eye = jnp.where(diag, 1.0, 0.0).astype(F32)
    gp = gp_ref[...]

    def stack_heads(rows, base):
        return jnp.concatenate([rows[:, base + h * Dh:base + (h + 1) * Dh] for h in range(H)], axis=0)

    def scalars(n):
        ba = ba_ref[0, n]
        beta_row = jax.nn.sigmoid(ba[0:1, :])
        g_row = -jnp.exp(gp[0:1, :]) * _softplus(ba[1:2, :] + gp[1:2, :])
        gc_col = jnp.sum(jnp.where(incl, jnp.broadcast_to(g_row, (HC, HC)), 0.0), axis=1, keepdims=True)
        beta_col = jnp.sum(jnp.where(diag, jnp.broadcast_to(beta_row, (HC, HC)), 0.0), axis=1, keepdims=True)
        gc_row = jnp.sum(jnp.where(diag, jnp.broadcast_to(gc_col, (HC, HC)), 0.0), axis=0, keepdims=True)
        decay = jnp.where(incl, jnp.exp(jnp.where(incl, gc_col - gc_row, 0.0)), 0.0)
        return gc_col, beta_col, decay

    def normalized(n):
        rows = qkv[n * C:(n + 1) * C, :]
        q = stack_heads(rows, 0)
        k = stack_heads(rows, GDN_WIDTH)
        v = stack_heads(rows, 2 * GDN_WIDTH)
        q = q * lax.rsqrt(jnp.sum(q * q, axis=-1, keepdims=True) + EPS) * (Dh ** -0.5)
        k = k * lax.rsqrt(jnp.sum(k * k, axis=-1, keepdims=True) + EPS)
        return q, k, v

    chunk_ids = range(GDN_CHUNKS_PER_STEP)
    qkv_n = [normalized(n) for n in chunk_ids]
    sc = [scalars(n) for n in chunk_ids]
    kk = [_mm_nt(k, k) for _, k, _ in qkv_n]
    p = [jnp.where(strict, -(beta_col * kk_n * decay), 0.0) for (_, beta_col, decay), kk_n in zip(sc, kk)]
    t_inv = [eye + p_n for p_n in p]
    for _ in range(5):
        p = [_mm(p_n, p_n) for p_n in p]
        t_inv = [t_n + _mm(t_n, p_n) for t_n, p_n in zip(t_inv, p)]
    exp_gc = [jnp.exp(gc_col) for gc_col, _, _ in sc]
    sol = [_mm(t_n, jnp.concatenate([v * beta_col, k * (beta_col * e_n)], axis=1))
           for t_n, (_, k, v), (_, beta_col, _), e_n in zip(t_inv, qkv_n, sc, exp_gc)]
    qk = [_mm_nt(q, k) * decay for (q, k, _), (_, _, decay) in zip(qkv_n, sc)]
    pre = [dict(u=sol[n][:, :Dh], wk=sol[n][:, Dh:], qk=qk[n], qe=qkv_n[n][0] * exp_gc[n],
                k=qkv_n[n][1], gc=sc[n][0]) for n in chunk_ids]

    for n in chunk_ids:
        c = pre[n]
        v_new, o_state = [], []
        for h in range(H):
            rows = slice(h * C, (h + 1) * C)
            ws = _mm(jnp.concatenate([c["wk"][rows], c["qe"][rows]], axis=0), state_ref[h])
            v_new.append(c["u"][rows] - ws[:C])
            o_state.append(ws[C:])
        v_new = jnp.concatenate(v_new, axis=0)
        o = jnp.concatenate(o_state, axis=0) + _mm(c["qk"], v_new)
        for h in range(H):
            rows = slice(h * C, (h + 1) * C)
            g_last = c["gc"][(h + 1) * C - 1:(h + 1) * C, :]
            k_dec = c["k"][rows] * jnp.exp(g_last - c["gc"][rows])
            state_ref[h] = state_ref[h] * jnp.exp(g_last) + _mm_tn(k_dec, v_new[rows])
        o_n = _rms(o, nw_ref[...])
        for h in range(H):
            zz = z_ref[0, n * C:(n + 1) * C, h * Dh:(h + 1) * Dh].astype(F32)
            out_ref[0, n * C:(n + 1) * C, h * Dh:(h + 1) * Dh] = (
                o_n[h * C:(h + 1) * C] * (zz * jax.nn.sigmoid(zz))).astype(BF16)


def _gdn(proj3, ba_rows, conv_w, gparams, norm_w):
    bsz, seq, _ = proj3.shape
    cps = GDN_CHUNKS_PER_STEP
    rows = cps * GDN_CHUNK
    prev_per_step = rows // PREV_ROWS
    return pl.pallas_call(
        _gdn_kernel,
        grid=(bsz, seq // rows),
        in_specs=[
            pl.BlockSpec((1, rows, 3 * GDN_WIDTH), lambda b, s: (b, s, 0)),
            pl.BlockSpec((1, PREV_ROWS, 3 * GDN_WIDTH),
                         lambda b, s: (b, jnp.maximum(s * prev_per_step - 1, 0), 0)),
            pl.BlockSpec((1, rows, GDN_WIDTH), lambda b, s: (b, s, 3)),
            pl.BlockSpec((1, cps, 2, GDN_HC), lambda b, s: (b, s, 0, 0)),
            pl.BlockSpec((GDN_CONV, 3 * GDN_WIDTH), lambda b, s: (0, 0)),
            pl.BlockSpec((2, GDN_HC), lambda b, s: (0, 0)),
            pl.BlockSpec((1, GDN_HEAD_DIM), lambda b, s: (0, 0)),
        ],
        out_specs=pl.BlockSpec((1, rows, GDN_WIDTH), lambda b, s: (b, s, 0)),
        out_shape=jax.ShapeDtypeStruct((bsz, seq, GDN_WIDTH), BF16),
        scratch_shapes=[pltpu.VMEM((GDN_HEADS, GDN_HEAD_DIM, GDN_HEAD_DIM), F32)],
        compiler_params=_cparams(2),
        name="gdn",
    )(proj3, proj3, proj3, ba_rows, conv_w, gparams, norm_w)


def _sb_kernel(q_ref, k_ref, v_ref, nw_ref, out_ref, acc_ref, later_ref):
    T, Dh = SB_TILE, SB_HEAD_DIM
    qb = pl.program_id(2)
    row = lax.broadcasted_iota(I32, (T, T), 0)
    col = lax.broadcasted_iota(I32, (T, T), 1)
    causal = col < row
    neg_suffix = jnp.where(row >= col, -1.0, 0.0).astype(BF16)
    neg_suffix2 = jnp.concatenate([neg_suffix, neg_suffix], axis=0)

    low = lax.broadcasted_iota(I32, (T, LANES), 1) < Dh
    q = q_ref[0] * jnp.asarray(Dh ** -0.5, BF16)
    zeros = jnp.zeros_like(q)
    q_heads = (jnp.where(low, q, zeros), jnp.where(low, zeros, q))

    def tiles(kbs, first_is_diagonal):
        blocks = []
        for n, kb in enumerate(kbs):
            start = pl.multiple_of(kb * T, T)
            blocks.append((k_ref[0, pl.ds(start, T), :], v_ref[0, pl.ds(start, T), :],
                           first_is_diagonal and n == 0))
        chains = [(i, k_blk, v_blk, on_diag) for i in range(2) for k_blk, v_blk, on_diag in blocks]
        zs = [lax.dot_general(q_heads[i], k_blk, (((1,), (1,)), ((), ())), preferred_element_type=F32)
              for i, k_blk, _, _ in chains]
        splits = []
        for (_, _, _, on_diag), z in zip(chains, zs):
            neg_log_remain = _softplus(z)
            if on_diag:
                neg_log_remain = jnp.where(causal, neg_log_remain, 0.0)
            splits.append(neg_log_remain.astype(BF16))
        incl_after = [jnp.dot(s, neg_suffix, preferred_element_type=F32) for s in splits]
        later = [later_ref[0], later_ref[1]]
        weights = []
        for (i, _, _, on_diag), z, incl in zip(chains, zs, incl_after):
            w = jnp.exp(z + incl + later[i])
            if on_diag:
                w = jnp.where(causal, w, 0.0)
            weights.append(w.astype(BF16))
            later[i] = later[i] + incl[:, 0:1]
        acc = [acc_ref[0], acc_ref[1]]
        for (i, _, v_blk, _), w in zip(chains, weights):
            acc[i] = acc[i] + jnp.dot(w, v_blk, preferred_element_type=F32)
        for i in range(2):
            later_ref[i] = later[i]
            acc_ref[i] = acc[i]

    later_ref[...] = jnp.zeros_like(later_ref)
    acc_ref[...] = jnp.zeros_like(acc_ref)

    head = (qb % 4) + 1

    @pl.when(head == 1)
    def _():
        tiles([qb], True)

    @pl.when(head == 2)
    def _():
        tiles([qb, qb - 1], True)

    @pl.when(head == 3)
    def _():
        tiles([qb], True)
        tiles([qb - 1, qb - 2], False)

    @pl.when(head == 4)
    def _():
        tiles([qb, qb - 1, qb - 2, qb - 3], True)

    def quad(i, carry):
        kb = qb - head - 4 * i
        tiles([kb, kb - 1, kb - 2, kb - 3], False)
        return carry

    lax.fori_loop(0, (qb + 1 - head) // 4, quad, 0)

    o = jnp.where(low, acc_ref[0], acc_ref[1])
    hr = lax.broadcasted_iota(I32, (LANES, LANES), 0) // Dh
    hc = lax.broadcasted_iota(I32, (LANES, LANES), 1) // Dh
    avg = jnp.where(hr == hc, 1.0 / Dh, 0.0).astype(BF16)
    hi, lo = _split(o * o)
    ms = jnp.dot(jnp.concatenate([hi, lo], axis=1), jnp.concatenate([avg, avg], axis=0),
                 preferred_element_type=F32)
    out_ref[0] = (o * lax.rsqrt(ms + EPS) * nw_ref[...]).astype(BF16)


def _sb(proj3, norm_w2):
    bsz, seq, _ = proj3.shape
    tile = min(SB_TILE, seq)
    assert tile == SB_TILE and seq % tile == 0
    pairs = SB_WIDTH // LANES
    q_col = (3 * GDN_WIDTH + GDN_WIDTH) // LANES
    k_col = q_col + pairs
    v_col = k_col + pairs
    return pl.pallas_call(
        _sb_kernel,
        grid=(bsz, pairs, seq // tile),
        in_specs=[
            pl.BlockSpec((1, tile, LANES), lambda b, hp, qb: (b, qb, q_col + hp)),
            pl.BlockSpec((1, seq, LANES), lambda b, hp, qb: (b, 0, k_col + hp)),
            pl.BlockSpec((1, seq, LANES), lambda b, hp, qb: (b, 0, v_col + hp)),
            pl.BlockSpec((1, LANES), lambda b, hp, qb: (0, 0)),
        ],
        out_specs=pl.BlockSpec((1, tile, LANES), lambda b, hp, qb: (b, qb, hp)),
        out_shape=jax.ShapeDtypeStruct((bsz, seq, SB_WIDTH), BF16),
        scratch_shapes=[pltpu.VMEM((2, tile, LANES), F32),
                        pltpu.VMEM((2, tile, 1), F32)],
        compiler_params=_cparams(3),
        name="sb_attn",
    )(proj3, proj3, proj3, norm_w2)


def _post_mix_kernel(x_ref, g_ref, s_ref, wo_ref, ln_ref, wrt_ref, br_ref,
                     x1_ref, h2r_ref, route_ref, gcol_ref, cnt_ref, count_ref):
    tm = x_ref.shape[0]
    i = pl.program_id(0)

    @pl.when(i == 0)
    def _():
        count_ref[...] = jnp.zeros_like(count_ref)

    x1 = (x_ref[...]
          + jnp.dot(g_ref[...], wo_ref[0:GDN_WIDTH, :], preferred_element_type=F32)
          + jnp.dot(s_ref[...], wo_ref[GDN_WIDTH:, :], preferred_element_type=F32))
    x1_ref[...] = x1
    h2 = _rms(x1, ln_ref[...])
    h2r_ref[...] = _rows_to_tiles(h2)

    w_hi, w_lo = _split(wrt_ref[...])
    h_hi, h_lo = _split(h2)
    nt = (((1,), (1,)), ((), ()))
    logits = (lax.dot_general(w_hi, h_hi, nt, preferred_element_type=F32)
              + lax.dot_general(w_hi, h_lo, nt, preferred_element_type=F32)
              + lax.dot_general(w_lo, h_hi, nt, preferred_element_type=F32)) + br_ref[:, 0:1]

    g_logits = logits[N_EXPERTS:N_EXPERTS + N_GROUPS, :]
    g_max = jnp.max(g_logits, axis=0, keepdims=True)
    group_w = 1.0 / jnp.sum(jnp.exp(g_logits - g_max), axis=0, keepdims=True)
    sub8 = lax.broadcasted_iota(I32, (N_GROUPS, tm), 0)
    g_idx = jnp.min(jnp.where(g_logits == g_max, sub8, N_GROUPS), axis=0, keepdims=True)

    e_logits = logits[0:N_EXPERTS, :]
    erow = lax.broadcasted_iota(I32, (N_EXPERTS, tm), 0)
    neg_inf = -jnp.inf
    in_group = jnp.where((erow // EXPERTS_PER_GROUP) == g_idx, e_logits, neg_inf)
    m1 = jnp.max(in_group, axis=0, keepdims=True)
    idx1 = jnp.min(jnp.where(in_group == m1, erow, N_EXPERTS), axis=0, keepdims=True)
    rest = jnp.where(erow == idx1, neg_inf, in_group)
    m2 = jnp.max(rest, axis=0, keepdims=True)
    idx2 = jnp.min(jnp.where(rest == m2, erow, N_EXPERTS), axis=0, keepdims=True)
    r = jnp.exp(m2 - m1)
    w1 = 1.0 / (1.0 + r)
    gate1 = group_w * w1
    gate2 = group_w * (r * w1)

    oh1 = erow == idx1
    oh2 = erow == idx2
    picked = jnp.where(oh1 | oh2, 1.0, 0.0)
    tr = lax.broadcasted_iota(I32, (tm, tm), 0)
    tc = lax.broadcasted_iota(I32, (tm, tm), 1)
    before = jnp.where(tr < tc, 1.0, 0.0).astype(BF16)
    seen = jnp.dot(picked.astype(BF16), before, preferred_element_type=F32) + count_ref[:, 0:1]
    rank1 = jnp.sum(jnp.where(oh1, seen, 0.0), axis=0, keepdims=True)
    rank2 = jnp.sum(jnp.where(oh2, seen, 0.0), axis=0, keepdims=True)
    count_ref[...] = count_ref[...] + jnp.sum(picked, axis=1, keepdims=True)
    cnt_ref[...] = count_ref[...]

    sub = lax.broadcasted_iota(I32, (SUBLANES, tm), 0)
    rows = (idx1.astype(F32), idx2.astype(F32), rank1, rank2)
    route = jnp.zeros((SUBLANES, tm), F32)
    for n, val in enumerate(rows):
        route = jnp.where(sub == n, val, route)
    route_ref[...] = route

    sub_l = lax.broadcasted_iota(I32, (LANES, tm), 0)
    gates_t = jnp.where(sub_l == 0, gate1, jnp.where(sub_l == 1, gate2, 0.0))
    gcol_ref[...] = gates_t.T


def _post_mix(x2d, gdn_out, sb_out, w_out, ln, w_router_t, b_router):
    n_tok = x2d.shape[0]
    tm = min(POST_TM, n_tok)
    return pl.pallas_call(
        _post_mix_kernel,
        grid=(n_tok // tm,),
        in_specs=[
            pl.BlockSpec((tm, D_MODEL), lambda i: (i, 0)),
            pl.BlockSpec((tm, GDN_WIDTH), lambda i: (i, 0)),
            pl.BlockSpec((tm, SB_WIDTH), lambda i: (i, 0)),
            pl.BlockSpec((D_MODEL, D_MODEL), lambda i: (0, 0)),
            pl.BlockSpec((1, D_MODEL), lambda i: (0, 0)),
            pl.BlockSpec((LANES, D_MODEL), lambda i: (0, 0)),
            pl.BlockSpec((LANES, LANES), lambda i: (0, 0)),
        ],
        out_specs=[
            pl.BlockSpec((tm, D_MODEL), lambda i: (i, 0)),
            pl.BlockSpec((tm, ROW_TILES, LANES), lambda i: (i, 0, 0)),
            pl.BlockSpec((SUBLANES, tm), lambda i: (0, i)),
            pl.BlockSpec((tm, LANES), lambda i: (i, 0)),
            pl.BlockSpec((N_EXPERTS, LANES), lambda i: (0, 0)),
        ],
        out_shape=[
            jax.ShapeDtypeStruct((n_tok, D_MODEL), F32),
            jax.ShapeDtypeStruct((n_tok, ROW_TILES, LANES), F32),
            jax.ShapeDtypeStruct((SUBLANES, n_tok), F32),
            jax.ShapeDtypeStruct((n_tok, LANES), F32),
            jax.ShapeDtypeStruct((N_EXPERTS, LANES), F32),
        ],
        scratch_shapes=[pltpu.VMEM((N_EXPERTS, LANES), F32)],
        compiler_params=_cparams(1),
        name="post_mix",
    )(x2d, gdn_out, sb_out, w_out, ln, w_router_t, b_router)


def _dispatch_kernel(d1_ref, d2_ref, rows_ref, buf_in_hbm, buf_hbm, sem):
    del buf_in_hbm
    tm = rows_ref.shape[0]

    def issue(t, carry):
        pltpu.make_async_copy(rows_ref.at[t], buf_hbm.at[d1_ref[0, 0, t]], sem).start(priority=0)
        pltpu.make_async_copy(rows_ref.at[t], buf_hbm.at[d2_ref[0, 0, t]], sem).start(priority=1)
        return carry

    lax.fori_loop(0, tm, issue, 0, unroll=DMA_ISSUE_UNROLL)
    for _ in range(2):
        pltpu.make_async_copy(rows_ref, buf_hbm.at[pl.ds(0, tm)], sem).wait()


def _dispatch(dest1, dest2, h2r, n_rows):
    n_tok = h2r.shape[0]
    tm = min(DISP_TM, n_tok)
    n_steps = n_tok // tm
    d1 = dest1.reshape(n_steps, 1, tm)
    d2 = dest2.reshape(n_steps, 1, tm)
    smem_spec = pl.BlockSpec((1, 1, tm), lambda i: (i, 0, 0), memory_space=pltpu.SMEM)
    buf0 = jnp.zeros((n_rows, ROW_TILES, LANES), F32)
    return pl.pallas_call(
        _dispatch_kernel,
        grid=(n_steps,),
        in_specs=[smem_spec, smem_spec,
                  pl.BlockSpec((tm, ROW_TILES, LANES), lambda i: (i, 0, 0)),
                  pl.BlockSpec(memory_space=pl.ANY)],
        out_specs=pl.BlockSpec(memory_space=pl.ANY),
        out_shape=jax.ShapeDtypeStruct((n_rows, ROW_TILES, LANES), F32),
        scratch_shapes=[pltpu.SemaphoreType.DMA(())],
        input_output_aliases={3: 0},
        compiler_params=_cparams(1),
        name="moe_dispatch",
    )(d1, d2, h2r, buf0)


def _ffn_kernel(be_ref, x_ref, wg_ref, wu_ref, wd_ref, y_ref, wgu_bf, wd_bf):
    i = pl.program_id(0)

    @pl.when((i == 0) | (be_ref[i] != be_ref[jnp.maximum(i - 1, 0)]))
    def _():
        wgu_bf[:, :D_EXPERT] = wg_ref[0].astype(BF16)
        wgu_bf[:, D_EXPERT:] = wu_ref[0].astype(BF16)
        wd_bf[...] = wd_ref[0].astype(BF16)

    acc = jnp.dot(_tiles_to_rows(x_ref[...]).astype(BF16), wgu_bf[...], preferred_element_type=F32)
    gate, up = acc[:, :D_EXPERT], acc[:, D_EXPERT:]
    hidden = (gate * jax.nn.sigmoid(gate)) * up
    y_ref[...] = _rows_to_tiles(jnp.dot(hidden.astype(BF16), wd_bf[...], preferred_element_type=F32))


def _ffn(block_expert, buf, w_gate, w_up, w_down):
    n_rows = buf.shape[0]
    n_blocks = n_rows // MOE_BLOCK
    grid_spec = pltpu.PrefetchScalarGridSpec(
        num_scalar_prefetch=1,
        grid=(n_blocks,),
        in_specs=[
            pl.BlockSpec((MOE_BLOCK, ROW_TILES, LANES), lambda i, be: (i, 0, 0)),
            pl.BlockSpec((1, D_MODEL, D_EXPERT), lambda i, be: (be[i], 0, 0)),
            pl.BlockSpec((1, D_MODEL, D_EXPERT), lambda i, be: (be[i], 0, 0)),
            pl.BlockSpec((1, D_EXPERT, D_MODEL), lambda i, be: (be[i], 0, 0)),
        ],
        out_specs=pl.BlockSpec((MOE_BLOCK, ROW_TILES, LANES), lambda i, be: (i, 0, 0)),
        scratch_shapes=[pltpu.VMEM((D_MODEL, 2 * D_EXPERT), BF16),
                        pltpu.VMEM((D_EXPERT, D_MODEL), BF16)],
    )
    return pl.pallas_call(
        _ffn_kernel,
        grid_spec=grid_spec,
        out_shape=jax.ShapeDtypeStruct((n_rows, ROW_TILES, LANES), F32),
        compiler_params=_cparams(1),
        name="moe_ffn",
    )(block_expert, buf, w_gate, w_up, w_down)


def _final_kernel(d1_ref, d2_ref, d1_next_ref, d2_next_ref, ybuf_hbm, x1_ref, gcol_ref, p_ref, wpg_ref,
                  wpp_ref, lnp_ref, lnf_ref, out_ref, y1_ref, y2_ref, sem):
    tm = x1_ref.shape[0]
    i = pl.program_id(0)
    slot = i % 2

    def gather(idx1_ref, idx2_ref, dst_slot):
        def issue(t, carry):
            pltpu.make_async_copy(ybuf_hbm.at[idx1_ref[0, 0, t]], y1_ref.at[dst_slot, t],
                                  sem.at[dst_slot]).start(priority=0)
            pltpu.make_async_copy(ybuf_hbm.at[idx2_ref[0, 0, t]], y2_ref.at[dst_slot, t],
                                  sem.at[dst_slot]).start(priority=1)
            return carry
        lax.fori_loop(0, tm, issue, 0, unroll=DMA_ISSUE_UNROLL)

    @pl.when(i == 0)
    def _():
        gather(d1_ref, d2_ref, 0)

    @pl.when(i + 1 < pl.num_programs(0))
    def _():
        gather(d1_next_ref, d2_next_ref, 1 - slot)

    pltpu.make_async_copy(ybuf_hbm.at[pl.ds(0, tm)], y1_ref.at[slot], sem.at[slot]).wait()
    pltpu.make_async_copy(ybuf_hbm.at[pl.ds(0, tm)], y2_ref.at[slot], sem.at[slot]).wait()

    g1 = gcol_ref[:, 0:1]
    g2 = gcol_ref[:, 1:2]
    x2 = x1_ref[...] + (g1 * _tiles_to_rows(y1_ref[slot]) + g2 * _tiles_to_rows(y2_ref[slot]))
    hp = _rms(x2, lnp_ref[...]).astype(BF16)
    ple_gate = jax.nn.sigmoid(jnp.dot(hp, wpg_ref[...], preferred_element_type=F32))
    ple = jnp.dot(p_ref[...].astype(BF16), wpp_ref[...], preferred_element_type=F32)
    x3 = x2 + ple_gate * ple
    out_ref[...] = _rms(x3, lnf_ref[...])


def _final(dest1, dest2, ybuf, x1, gcol, p2d, w_ple_gate, w_ple_proj, ln_ple, ln_final):
    n_tok = x1.shape[0]
    tm = min(FINAL_TM, n_tok)
    n_steps = n_tok // tm
    d1 = dest1.reshape(n_steps, 1, tm)
    d2 = dest2.reshape(n_steps, 1, tm)
    smem_spec = pl.BlockSpec((1, 1, tm), lambda i: (i, 0, 0), memory_space=pltpu.SMEM)
    smem_next = pl.BlockSpec((1, 1, tm), lambda i: (jnp.minimum(i + 1, n_steps - 1), 0, 0),
                             memory_space=pltpu.SMEM)
    return pl.pallas_call(
        _final_kernel,
        grid=(n_steps,),
        in_specs=[
            smem_spec, smem_spec, smem_next, smem_next,
            pl.BlockSpec(memory_space=pl.ANY),
            pl.BlockSpec((tm, D_MODEL), lambda i: (i, 0)),
            pl.BlockSpec((tm, LANES), lambda i: (i, 0)),
            pl.BlockSpec((tm, PLE_DIM), lambda i: (i, 0)),
            pl.BlockSpec((D_MODEL, D_MODEL), lambda i: (0, 0)),
            pl.BlockSpec((PLE_DIM, D_MODEL), lambda i: (0, 0)),
            pl.BlockSpec((1, D_MODEL), lambda i: (0, 0)),
            pl.BlockSpec((1, D_MODEL), lambda i: (0, 0)),
        ],
        out_specs=pl.BlockSpec((tm, D_MODEL), lambda i: (i, 0)),
        out_shape=jax.ShapeDtypeStruct((n_tok, D_MODEL), F32),
        scratch_shapes=[pltpu.VMEM((2, tm, ROW_TILES, LANES), F32),
                        pltpu.VMEM((2, tm, ROW_TILES, LANES), F32),
                        pltpu.SemaphoreType.DMA((2,))],
        compiler_params=_cparams(1),
        name="moe_combine_ple_final",
    )(d1, d2, d1, d2, ybuf, x1, gcol, p2d, w_ple_gate, w_ple_proj, ln_ple, ln_final)


def _layer(x2d, p2d, bsz, seq, ln_mix, w_in, conv_w, a_log, dt_bias, gdn_norm_w, sb_norm_w, w_out,
           ln_moe, w_rg, b_rg, w_re, b_re, w_eg, w_eu, w_ed, ln_ple, w_ple_gate, w_ple_proj, ln_final):
    n_tok = bsz * seq
    C, H = GDN_CHUNK, GDN_HEADS
    o_z = 3 * GDN_WIDTH
    o_b = o_z + GDN_WIDTH
    o_q = o_b + 2 * H
    w_all = jnp.concatenate(
        [w_in[:, :o_b], w_in[:, o_q:], w_in[:, o_b:o_q],
         jnp.zeros((D_MODEL, LANES - 2 * H), F32)], axis=1).astype(BF16)
    proj, ba = _in_proj(x2d, ln_mix.reshape(1, D_MODEL), w_all)
    proj3 = proj.reshape(bsz, seq, PROJ_W)
    ba_rows = (ba[:, :2 * H].reshape(bsz, seq // C, C, 2, H)
               .transpose(0, 1, 3, 4, 2).reshape(bsz, seq // C, 2, GDN_HC))
    gparams = jnp.repeat(jnp.stack([a_log, dt_bias]).astype(F32), C, axis=1)

    gdn_out = _gdn(proj3, ba_rows, conv_w, gparams, gdn_norm_w.reshape(1, GDN_HEAD_DIM))
    sb_out = _sb(proj3, jnp.tile(sb_norm_w, LANES // SB_HEAD_DIM).reshape(1, LANES))

    w_router_t = jnp.zeros((LANES, D_MODEL), F32)
    w_router_t = w_router_t.at[:N_EXPERTS].set(w_re.T).at[N_EXPERTS:N_EXPERTS + N_GROUPS].set(w_rg.T)
    b_router = jnp.zeros((LANES,), F32).at[:N_EXPERTS].set(b_re).at[N_EXPERTS:N_EXPERTS + N_GROUPS].set(b_rg)
    b_router = jnp.broadcast_to(b_router[:, None], (LANES, LANES))
    x1, h2r, route, gcol, cnt = _post_mix(
        x2d, gdn_out.reshape(n_tok, GDN_WIDTH), sb_out.reshape(n_tok, SB_WIDTH),
        w_out.astype(BF16), ln_moe.reshape(1, D_MODEL), w_router_t, b_router)

    counts = cnt[:, 0].astype(I32)
    padded = (counts + MOE_BLOCK - 1) // MOE_BLOCK * MOE_BLOCK
    padded_end = jnp.cumsum(padded)
    padded_start = padded_end - padded
    route_i = route[:4].astype(I32)
    dest1 = padded_start[route_i[0]] + route_i[2]
    dest2 = padded_start[route_i[1]] + route_i[3]
    n_blocks = (2 * n_tok + MOE_BLOCK - 1) // MOE_BLOCK + N_EXPERTS
    block_row0 = jnp.arange(n_blocks, dtype=I32) * MOE_BLOCK
    block_expert = jnp.minimum(
        jnp.sum((padded_end[:, None] <= block_row0[None, :]).astype(I32), axis=0), N_EXPERTS - 1)

    buf = _dispatch(dest1, dest2, h2r, n_blocks * MOE_BLOCK)
    ybuf = _ffn(block_expert, buf, w_eg, w_eu, w_ed)
    return _final(dest1, dest2, ybuf, x1, gcol, p2d, w_ple_gate.astype(BF16), w_ple_proj.astype(BF16),
                  ln_ple.reshape(1, D_MODEL), ln_final.reshape(1, D_MODEL))


def kernel(x, p, ln_mix, w_in, conv_w, a_log, dt_bias, gdn_norm_w, sb_norm_w, w_out, ln_moe,
           w_router_group, b_router_group, w_router_expert, b_router_expert, w_expert_gate,
           w_expert_up, w_expert_down, ln_ple, w_ple_gate, w_ple_proj, ln_final):
    bsz, seq, _ = x.shape
    depth = p.shape[0]
    assert depth == 1, "the final RMSNorm is fused into the layer's last kernel"
    x2d = x.reshape(bsz * seq, D_MODEL)
    out = _layer(x2d, p[0].reshape(bsz * seq, PLE_DIM), bsz, seq, ln_mix[0], w_in[0], conv_w[0],
                 a_log[0], dt_bias[0], gdn_norm_w[0], sb_norm_w[0], w_out[0], ln_moe[0],
                 w_router_group[0], b_router_group[0], w_router_expert[0], b_router_expert[0],
                 w_expert_gate[0], w_expert_up[0], w_expert_down[0], ln_ple[0], w_ple_gate[0],
                 w_ple_proj[0], ln_final)
    return out.reshape(bsz, seq, D_MODEL)
```

```python
import jax
import jax.numpy as jnp
from jax import lax
from jax.experimental import pallas as pl
from jax.experimental.pallas import tpu as pltpu

F32 = jnp.float32
BF16 = jnp.bfloat16
I32 = jnp.int32

D_MODEL = 1024
PLE_DIM = 256
GDN_HEADS = 4
GDN_HEAD_DIM = 128
GDN_WIDTH = GDN_HEADS * GDN_HEAD_DIM
GDN_CONV = 4
GDN_CHUNK = 64
SB_HEADS = 8
SB_HEAD_DIM = 64
SB_WIDTH = SB_HEADS * SB_HEAD_DIM
N_GROUPS = 8
EXPERTS_PER_GROUP = 8
N_EXPERTS = N_GROUPS * EXPERTS_PER_GROUP
D_EXPERT = D_MODEL // 4
MOE_BLOCK = 256
EPS = 1e-6

LANES = 128
SUBLANES = 8
ROW_TILES = D_MODEL // LANES
PROJ_W = 3 * GDN_WIDTH + GDN_WIDTH + 3 * SB_WIDTH
PROJ_CHUNK = 512
PREV_ROWS = 16
VMEM_LIMIT = 48 * 1024 * 1024

IN_TM = 512
GDN_CHUNKS_PER_STEP = 4
GDN_HC = GDN_HEADS * GDN_CHUNK
SB_TILE = 256
POST_TM = 512
DEST_TM = 2048
DISP_TM = 256
FINAL_TM = 512
DMA_ISSUE_UNROLL = 8


def _cparams(n_axes):
    return pltpu.CompilerParams(dimension_semantics=("arbitrary",) * n_axes,
                                vmem_limit_bytes=VMEM_LIMIT)


def _rms(x, w):
    return x * lax.rsqrt(jnp.mean(x * x, axis=-1, keepdims=True) + EPS) * w


def _mm(a, b):
    return jnp.dot(a.astype(BF16), b.astype(BF16), preferred_element_type=F32)


def _mm_nt(a, b):
    return lax.dot_general(a.astype(BF16), b.astype(BF16), (((1,), (1,)), ((), ())),
                           preferred_element_type=F32)


def _mm_tn(a, b):
    return lax.dot_general(a.astype(BF16), b.astype(BF16), (((0,), (0,)), ((), ())),
                           preferred_element_type=F32)


def _split(a):
    hi = a.astype(BF16)
    lo = (a - hi.astype(F32)).astype(BF16)
    return hi, lo


LOG2_E = 1.4426950408889634


def _softplus(x):
    return jnp.maximum(x, 0.0) + jnp.log(1.0 + jnp.exp2(jnp.abs(x) * (-LOG2_E)))


def _tiles_to_rows(x3):
    xt = jnp.transpose(x3, (1, 0, 2))
    return jnp.concatenate([xt[j] for j in range(ROW_TILES)], axis=1)


def _rows_to_tiles(x2):
    xt = jnp.stack([x2[:, j * LANES:(j + 1) * LANES] for j in range(ROW_TILES)], axis=0)
    return jnp.transpose(xt, (1, 0, 2))


def _in_proj_kernel(x_ref, ln_ref, w_ref, proj_ref, ba_ref):
    h = _rms(x_ref[...], ln_ref[...]).astype(BF16)
    for j in range(0, PROJ_W, PROJ_CHUNK):
        proj_ref[:, j:j + PROJ_CHUNK] = jnp.dot(
            h, w_ref[:, j:j + PROJ_CHUNK], preferred_element_type=F32).astype(BF16)
    ba_ref[...] = jnp.dot(h, w_ref[:, PROJ_W:PROJ_W + LANES], preferred_element_type=F32)


def _in_proj(x2d, ln, w_all):
    n_tok = x2d.shape[0]
    tm = min(IN_TM, n_tok)
    return pl.pallas_call(
        _in_proj_kernel,
        grid=(n_tok // tm,),
        in_specs=[pl.BlockSpec((tm, D_MODEL), lambda i: (i, 0)),
                  pl.BlockSpec((1, D_MODEL), lambda i: (0, 0)),
                  pl.BlockSpec((D_MODEL, PROJ_W + LANES), lambda i: (0, 0),
                               pipeline_mode=pl.Buffered(1))],
        out_specs=[pl.BlockSpec((tm, PROJ_W), lambda i: (i, 0)),
                   pl.BlockSpec((tm, LANES), lambda i: (i, 0))],
        out_shape=[jax.ShapeDtypeStruct((n_tok, PROJ_W), BF16),
                   jax.ShapeDtypeStruct((n_tok, LANES), F32)],
        compiler_params=_cparams(1),
        name="in_proj",
    )(x2d, ln, w_all)


def _gdn_kernel(cur_ref, prev_ref, z_ref, ba_ref, convw_ref, gp_ref, nw_ref, out_ref, state_ref):
    C, H, Dh, HC = GDN_CHUNK, GDN_HEADS, GDN_HEAD_DIM, GDN_HC
    rows_per_step = GDN_CHUNKS_PER_STEP * C
    step = pl.program_id(1)

    @pl.when(step == 0)
    def _():
        state_ref[...] = jnp.zeros_like(state_ref)

    cur = cur_ref[0].astype(F32)
    prev = jnp.where(step > 0, prev_ref[0].astype(F32), 0.0)
    stacked = jnp.concatenate([prev, cur], axis=0)
    w = convw_ref[...]
    y = w[GDN_CONV - 1:GDN_CONV, :] * cur
    for k in range(GDN_CONV - 1):
        shift = GDN_CONV - 1 - k
        y = y + w[k:k + 1, :] * pltpu.roll(stacked, shift, axis=0)[PREV_ROWS:, :]
    qkv = y * jax.nn.sigmoid(y)

    ri = lax.broadcasted_iota(I32, (HC, HC), 0)
    ci = lax.broadcasted_iota(I32, (HC, HC), 1)
    same_head = (ri // C) == (ci // C)
    incl = same_head & (ci <= ri)
    strict = same_head & (ci < ri)
    diag = ci == ri
    eye = jnp.where(diag, 1.0, 0.0).astype(F32)
    gp = gp_ref[...]

    def stack_heads(rows, base):
        return jnp.concatenate([rows[:, base + h * Dh:base + (h + 1) * Dh] for h in range(H)], axis=0)

    def scalars(n):
        ba = ba_ref[0, n]
        beta_row = jax.nn.sigmoid(ba[0:1, :])
        g_row = -jnp.exp(gp[0:1, :]) * _softplus(ba[1:2, :] + gp[1:2, :])
        gc_col = jnp.sum(jnp.where(incl, jnp.broadcast_to(g_row, (HC, HC)), 0.0), axis=1, keepdims=True)
        beta_col = jnp.sum(jnp.where(diag, jnp.broadcast_to(beta_row, (HC, HC)), 0.0), axis=1, keepdims=True)
        gc_row = jnp.sum(jnp.where(diag, jnp.broadcast_to(gc_col, (HC, HC)), 0.0), axis=0, keepdims=True)
        decay = jnp.where(incl, jnp.exp(jnp.where(incl, gc_col - gc_row, 0.0)), 0.0)
        return gc_col, beta_col, decay

    def normalized(n):
        rows = qkv[n * C:(n + 1) * C, :]
        q = stack_heads(rows, 0)
        k = stack_heads(rows, GDN_WIDTH)
        v = stack_heads(rows, 2 * GDN_WIDTH)
        q = q * lax.rsqrt(jnp.sum(q * q, axis=-1, keepdims=True) + EPS) * (Dh ** -0.5)
        k = k * lax.rsqrt(jnp.sum(k * k, axis=-1, keepdims=True) + EPS)
        return q, k, v

    chunk_ids = range(GDN_CHUNKS_PER_STEP)
    qkv_n = [normalized(n) for n in chunk_ids]
    sc = [scalars(n) for n in chunk_ids]
    kk = [_mm_nt(k, k) for _, k, _ in qkv_n]
    p = [jnp.where(strict, -(beta_col * kk_n * decay), 0.0) for (_, beta_col, decay), kk_n in zip(sc, kk)]
    t_inv = [eye + p_n for p_n in p]
    p = [_mm(p_n, p_n) for p_n in p]
    for level in range(5):
        if level < 4:
            both = [_mm(jnp.concatenate([t_n, p_n], axis=0), p_n) for t_n, p_n in zip(t_inv, p)]
            t_inv = [t_n + b_n[:HC] for t_n, b_n in zip(t_inv, both)]
            p = [b_n[HC:] for b_n in both]
        else:
            t_inv = [t_n + _mm(t_n, p_n) for t_n, p_n in zip(t_inv, p)]
    exp_gc = [jnp.exp(gc_col) for gc_col, _, _ in sc]
    sol = [_mm(t_n, jnp.concatenate([v * beta_col, k * (beta_col * e_n)], axis=1))
           for t_n, (_, k, v), (_, beta_col, _), e_n in zip(t_inv, qkv_n, sc, exp_gc)]
    qk = [_mm_nt(q, k) * decay for (q, k, _), (_, _, decay) in zip(qkv_n, sc)]
    pre = [dict(u=sol[n][:, :Dh], wk=sol[n][:, Dh:], qk=qk[n], qe=qkv_n[n][0] * exp_gc[n],
                k=qkv_n[n][1], gc=sc[n][0]) for n in chunk_ids]

    for n in chunk_ids:
        c = pre[n]
        v_new, o_state = [], []
        for h in range(H):
            rows = slice(h * C, (h + 1) * C)
            ws = _mm(jnp.concatenate([c["wk"][rows], c["qe"][rows]], axis=0), state_ref[h])
            v_new.append(c["u"][rows] - ws[:C])
            o_state.append(ws[C:])
        v_new = jnp.concatenate(v_new, axis=0)
        o = jnp.concatenate(o_state, axis=0) + _mm(c["qk"], v_new)
        for h in range(H):
            rows = slice(h * C, (h + 1) * C)
            g_last = c["gc"][(h + 1) * C - 1:(h + 1) * C, :]
            k_dec = c["k"][rows] * jnp.exp(g_last - c["gc"][rows])
            state_ref[h] = state_ref[h] * jnp.exp(g_last) + _mm_tn(k_dec, v_new[rows])
        o_n = _rms(o, nw_ref[...])
        for h in range(H):
            zz = z_ref[0, n * C:(n + 1) * C, h * Dh:(h + 1) * Dh].astype(F32)
            out_ref[0, n * C:(n + 1) * C, h * Dh:(h + 1) * Dh] = (
                o_n[h * C:(h + 1) * C] * (zz * jax.nn.sigmoid(zz))).astype(BF16)


def _gdn(proj3, ba_rows, conv_w, gparams, norm_w):
    bsz, seq, _ = proj3.shape
    cps = GDN_CHUNKS_PER_STEP
    rows = cps * GDN_CHUNK
    prev_per_step = rows // PREV_ROWS
    return pl.pallas_call(
        _gdn_kernel,
        grid=(bsz, seq // rows),
        in_specs=[
            pl.BlockSpec((1, rows, 3 * GDN_WIDTH), lambda b, s: (b, s, 0)),
            pl.BlockSpec((1, PREV_ROWS, 3 * GDN_WIDTH),
                         lambda b, s: (b, jnp.maximum(s * prev_per_step - 1, 0), 0)),
            pl.BlockSpec((1, rows, GDN_WIDTH), lambda b, s: (b, s, 3)),
            pl.BlockSpec((1, cps, 2, GDN_HC), lambda b, s: (b, s, 0, 0)),
            pl.BlockSpec((GDN_CONV, 3 * GDN_WIDTH), lambda b, s: (0, 0)),
            pl.BlockSpec((2, GDN_HC), lambda b, s: (0, 0)),
            pl.BlockSpec((1, GDN_HEAD_DIM), lambda b, s: (0, 0)),
        ],
        out_specs=pl.BlockSpec((1, rows, GDN_WIDTH), lambda b, s: (b, s, 0)),
        out_shape=jax.ShapeDtypeStruct((bsz, seq, GDN_WIDTH), BF16),
        scratch_shapes=[pltpu.VMEM((GDN_HEADS, GDN_HEAD_DIM, GDN_HEAD_DIM), F32)],
        compiler_params=_cparams(2),
        name="gdn",
    )(proj3, proj3, proj3, ba_rows, conv_w, gparams, norm_w)


def _sb_kernel(q_ref, k_ref, v_ref, nw_ref, out_ref, acc_ref, later_ref):
    T, Dh = SB_TILE, SB_HEAD_DIM
    qb = pl.program_id(2)
    row = lax.broadcasted_iota(I32, (T, T), 0)
    col = lax.broadcasted_iota(I32, (T, T), 1)
    causal = col < row
    neg_suffix = jnp.where(row >= col, -1.0, 0.0).astype(BF16)
    neg_suffix2 = jnp.concatenate([neg_suffix, neg_suffix], axis=0)

    low = lax.broadcasted_iota(I32, (T, LANES), 1) < Dh
    q = q_ref[0] * jnp.asarray(Dh ** -0.5, BF16)
    zeros = jnp.zeros_like(q)
    q_heads = (jnp.where(low, q, zeros), jnp.where(low, zeros, q))

    def tiles(kbs, first_is_diagonal):
        blocks = []
        for n, kb in enumerate(kbs):
            start = pl.multiple_of(kb * T, T)
            blocks.append((k_ref[0, pl.ds(start, T), :], v_ref[0, pl.ds(start, T), :],
                           first_is_diagonal and n == 0))
        chains = [(i, k_blk, v_blk, on_diag) for i in range(2) for k_blk, v_blk, on_diag in blocks]
        zs = [lax.dot_general(q_heads[i], k_blk, (((1,), (1,)), ((), ())), preferred_element_type=F32)
              for i, k_blk, _, _ in chains]
        splits = []
        for (_, _, _, on_diag), z in zip(chains, zs):
            neg_log_remain = _softplus(z)
            if on_diag:
                neg_log_remain = jnp.where(causal, neg_log_remain, 0.0)
            splits.append(neg_log_remain.astype(BF16))
        incl_all = jnp.dot(jnp.concatenate(splits, axis=0), neg_suffix, preferred_element_type=F32)
        incl_after = [incl_all[n * T:(n + 1) * T] for n in range(len(chains))]
        later = [later_ref[0], later_ref[1]]
        weights = []
        for (i, _, _, on_diag), z, incl in zip(chains, zs, incl_after):
            w = jnp.exp(z + incl + later[i])
            if on_diag:
                w = jnp.where(causal, w, 0.0)
            weights.append(w.astype(BF16))
            later[i] = later[i] + incl[:, 0:1]
        acc = [acc_ref[0], acc_ref[1]]
        for (i, _, v_blk, _), w in zip(chains, weights):
            acc[i] = acc[i] + jnp.dot(w, v_blk, preferred_element_type=F32)
        for i in range(2):
            later_ref[i] = later[i]
            acc_ref[i] = acc[i]

    later_ref[...] = jnp.zeros_like(later_ref)
    acc_ref[...] = jnp.zeros_like(acc_ref)

    head = (qb % 4) + 1

    @pl.when(head == 1)
    def _():
        tiles([qb], True)

    @pl.when(head == 2)
    def _():
        tiles([qb, qb - 1], True)

    @pl.when(head == 3)
    def _():
        tiles([qb], True)
        tiles([qb - 1, qb - 2], False)

    @pl.when(head == 4)
    def _():
        tiles([qb, qb - 1, qb - 2, qb - 3], True)

    def quad(i, carry):
        kb = qb - head - 4 * i
        tiles([kb, kb - 1, kb - 2, kb - 3], False)
        return carry

    lax.fori_loop(0, (qb + 1 - head) // 4, quad, 0)

    o = jnp.where(low, acc_ref[0], acc_ref[1])
    hr = lax.broadcasted_iota(I32, (LANES, LANES), 0) // Dh
    hc = lax.broadcasted_iota(I32, (LANES, LANES), 1) // Dh
    avg = jnp.where(hr == hc, 1.0 / Dh, 0.0).astype(BF16)
    hi, lo = _split(o * o)
    ms = jnp.dot(jnp.concatenate([hi, lo], axis=1), jnp.concatenate([avg, avg], axis=0),
                 preferred_element_type=F32)
    out_ref[0] = (o * lax.rsqrt(ms + EPS) * nw_ref[...]).astype(BF16)


def _sb(proj3, norm_w2):
    bsz, seq, _ = proj3.shape
    tile = min(SB_TILE, seq)
    assert tile == SB_TILE and seq % tile == 0
    pairs = SB_WIDTH // LANES
    q_col = (3 * GDN_WIDTH + GDN_WIDTH) // LANES
    k_col = q_col + pairs
    v_col = k_col + pairs
    return pl.pallas_call(
        _sb_kernel,
        grid=(bsz, pairs, seq // tile),
        in_specs=[
            pl.BlockSpec((1, tile, LANES), lambda b, hp, qb: (b, qb, q_col + hp)),
            pl.BlockSpec((1, seq, LANES), lambda b, hp, qb: (b, 0, k_col + hp)),
            pl.BlockSpec((1, seq, LANES), lambda b, hp, qb: (b, 0, v_col + hp)),
            pl.BlockSpec((1, LANES), lambda b, hp, qb: (0, 0)),
        ],
        out_specs=pl.BlockSpec((1, tile, LANES), lambda b, hp, qb: (b, qb, hp)),
        out_shape=jax.ShapeDtypeStruct((bsz, seq, SB_WIDTH), BF16),
        scratch_shapes=[pltpu.VMEM((2, tile, LANES), F32),
                        pltpu.VMEM((2, tile, 1), F32)],
        compiler_params=_cparams(3),
        name="sb_attn",
    )(proj3, proj3, proj3, norm_w2)


def _post_mix_kernel(x_ref, g_ref, s_ref, wo_ref, ln_ref, wrt_ref, br_ref,
                     x1_ref, h2r_ref, route_ref, gcol_ref, cnt_ref, count_ref):
    tm = x_ref.shape[0]
    i = pl.program_id(0)

    @pl.when(i == 0)
    def _():
        count_ref[...] = jnp.zeros_like(count_ref)

    x1 = (x_ref[...]
          + jnp.dot(g_ref[...], wo_ref[0:GDN_WIDTH, :], preferred_element_type=F32)
          + jnp.dot(s_ref[...], wo_ref[GDN_WIDTH:, :], preferred_element_type=F32))
    x1_ref[...] = x1
    h2 = _rms(x1, ln_ref[...])
    h2r_ref[...] = _rows_to_tiles(h2)

    w_hi, w_lo = _split(wrt_ref[...])
    h_hi, h_lo = _split(h2)
    nt = (((1,), (1,)), ((), ()))
    logits = (lax.dot_general(w_hi, h_hi, nt, preferred_element_type=F32)
              + lax.dot_general(w_hi, h_lo, nt, preferred_element_type=F32)
              + lax.dot_general(w_lo, h_hi, nt, preferred_element_type=F32)) + br_ref[:, 0:1]

    g_logits = logits[N_EXPERTS:N_EXPERTS + N_GROUPS, :]
    g_max = jnp.max(g_logits, axis=0, keepdims=True)
    group_w = 1.0 / jnp.sum(jnp.exp(g_logits - g_max), axis=0, keepdims=True)
    sub8 = lax.broadcasted_iota(I32, (N_GROUPS, tm), 0)
    g_idx = jnp.min(jnp.where(g_logits == g_max, sub8, N_GROUPS), axis=0, keepdims=True)

    e_logits = logits[0:N_EXPERTS, :]
    erow = lax.broadcasted_iota(I32, (N_EXPERTS, tm), 0)
    neg_inf = -jnp.inf
    in_group = jnp.where((erow // EXPERTS_PER_GROUP) == g_idx, e_logits, neg_inf)
    m1 = jnp.max(in_group, axis=0, keepdims=True)
    idx1 = jnp.min(jnp.where(in_group == m1, erow, N_EXPERTS), axis=0, keepdims=True)
    rest = jnp.where(erow == idx1, neg_inf, in_group)
    m2 = jnp.max(rest, axis=0, keepdims=True)
    idx2 = jnp.min(jnp.where(rest == m2, erow, N_EXPERTS), axis=0, keepdims=True)
    r = jnp.exp(m2 - m1)
    w1 = 1.0 / (1.0 + r)
    gate1 = group_w * w1
    gate2 = group_w * (r * w1)

    oh1 = erow == idx1
    oh2 = erow == idx2
    picked = jnp.where(oh1 | oh2, 1.0, 0.0)
    tr = lax.broadcasted_iota(I32, (tm, tm), 0)
    tc = lax.broadcasted_iota(I32, (tm, tm), 1)
    before = jnp.where(tr < tc, 1.0, 0.0).astype(BF16)
    seen = jnp.dot(picked.astype(BF16), before, preferred_element_type=F32) + count_ref[:, 0:1]
    rank1 = jnp.sum(jnp.where(oh1, seen, 0.0), axis=0, keepdims=True)
    rank2 = jnp.sum(jnp.where(oh2, seen, 0.0), axis=0, keepdims=True)
    count_ref[...] = count_ref[...] + jnp.sum(picked, axis=1, keepdims=True)
    cnt_ref[...] = count_ref[...]

    sub = lax.broadcasted_iota(I32, (SUBLANES, tm), 0)
    rows = (idx1.astype(F32), idx2.astype(F32), rank1, rank2)
    route = jnp.zeros((SUBLANES, tm), F32)
    for n, val in enumerate(rows):
        route = jnp.where(sub == n, val, route)
    route_ref[...] = route

    sub_l = lax.broadcasted_iota(I32, (LANES, tm), 0)
    gates_t = jnp.where(sub_l == 0, gate1, jnp.where(sub_l == 1, gate2, 0.0))
    gcol_ref[...] = gates_t.T


def _post_mix(x2d, gdn_out, sb_out, w_out, ln, w_router_t, b_router):
    n_tok = x2d.shape[0]
    tm = min(POST_TM, n_tok)
    return pl.pallas_call(
        _post_mix_kernel,
        grid=(n_tok // tm,),
        in_specs=[
            pl.BlockSpec((tm, D_MODEL), lambda i: (i, 0)),
            pl.BlockSpec((tm, GDN_WIDTH), lambda i: (i, 0)),
            pl.BlockSpec((tm, SB_WIDTH), lambda i: (i, 0)),
            pl.BlockSpec((D_MODEL, D_MODEL), lambda i: (0, 0)),
            pl.BlockSpec((1, D_MODEL), lambda i: (0, 0)),
            pl.BlockSpec((LANES, D_MODEL), lambda i: (0, 0)),
            pl.BlockSpec((LANES, LANES), lambda i: (0, 0)),
        ],
        out_specs=[
            pl.BlockSpec((tm, D_MODEL), lambda i: (i, 0)),
            pl.BlockSpec((tm, ROW_TILES, LANES), lambda i: (i, 0, 0)),
            pl.BlockSpec((SUBLANES, tm), lambda i: (0, i)),
            pl.BlockSpec((tm, LANES), lambda i: (i, 0)),
            pl.BlockSpec((N_EXPERTS, LANES), lambda i: (0, 0)),
        ],
        out_shape=[
            jax.ShapeDtypeStruct((n_tok, D_MODEL), F32),
            jax.ShapeDtypeStruct((n_tok, ROW_TILES, LANES), F32),
            jax.ShapeDtypeStruct((SUBLANES, n_tok), F32),
            jax.ShapeDtypeStruct((n_tok, LANES), F32),
            jax.ShapeDtypeStruct((N_EXPERTS, LANES), F32),
        ],
        scratch_shapes=[pltpu.VMEM((N_EXPERTS, LANES), F32)],
        compiler_params=_cparams(1),
        name="post_mix",
    )(x2d, gdn_out, sb_out, w_out, ln, w_router_t, b_router)


def _dest_kernel(route_ref, start_ref, dest_ref):
    tm = route_ref.shape[1]
    erow = lax.broadcasted_iota(I32, (N_EXPERTS, tm), 0).astype(F32)
    starts = start_ref[:, 0:1]
    sub = lax.broadcasted_iota(I32, (SUBLANES, tm), 0)
    dest = jnp.zeros((SUBLANES, tm), F32)
    for n in range(2):
        expert = route_ref[n:n + 1, :]
        rank = route_ref[2 + n:3 + n, :]
        row = jnp.sum(jnp.where(erow == expert, starts, 0.0), axis=0, keepdims=True) + rank
        dest = jnp.where(sub == n, row, dest)
    dest_ref[...] = dest.astype(I32)


def _dest(route, padded_start):
    n_tok = route.shape[1]
    tm = min(DEST_TM, n_tok)
    starts = jnp.broadcast_to(padded_start.astype(F32)[:, None], (N_EXPERTS, LANES))
    return pl.pallas_call(
        _dest_kernel,
        grid=(n_tok // tm,),
        in_specs=[pl.BlockSpec((SUBLANES, tm), lambda i: (0, i)),
                  pl.BlockSpec((N_EXPERTS, LANES), lambda i: (0, 0))],
        out_specs=pl.BlockSpec((SUBLANES, tm), lambda i: (0, i)),
        out_shape=jax.ShapeDtypeStruct((SUBLANES, n_tok), I32),
        compiler_params=_cparams(1),
        name="moe_dest",
    )(route, starts)


def _dispatch_kernel(d1_ref, d2_ref, rows_ref, buf_in_hbm, buf_hbm, sem):
    del buf_in_hbm
    tm = rows_ref.shape[0]

    def issue(t, carry):
        pltpu.make_async_copy(rows_ref.at[t], buf_hbm.at[d1_ref[0, 0, t]], sem).start(priority=0)
        pltpu.make_async_copy(rows_ref.at[t], buf_hbm.at[d2_ref[0, 0, t]], sem).start(priority=1)
        return carry

    lax.fori_loop(0, tm, issue, 0, unroll=DMA_ISSUE_UNROLL)
    for _ in range(2):
        pltpu.make_async_copy(rows_ref, buf_hbm.at[pl.ds(0, tm)], sem).wait()


def _dispatch(dest1, dest2, h2r, n_rows):
    n_tok = h2r.shape[0]
    tm = min(DISP_TM, n_tok)
    n_steps = n_tok // tm
    d1 = dest1.reshape(n_steps, 1, tm)
    d2 = dest2.reshape(n_steps, 1, tm)
    smem_spec = pl.BlockSpec((1, 1, tm), lambda i: (i, 0, 0), memory_space=pltpu.SMEM)
    buf0 = jnp.zeros((n_rows, ROW_TILES, LANES), F32)
    return pl.pallas_call(
        _dispatch_kernel,
        grid=(n_steps,),
        in_specs=[smem_spec, smem_spec,
                  pl.BlockSpec((tm, ROW_TILES, LANES), lambda i: (i, 0, 0)),
                  pl.BlockSpec(memory_space=pl.ANY)],
        out_specs=pl.BlockSpec(memory_space=pl.ANY),
        out_shape=jax.ShapeDtypeStruct((n_rows, ROW_TILES, LANES), F32),
        scratch_shapes=[pltpu.SemaphoreType.DMA(())],
        input_output_aliases={3: 0},
        compiler_params=_cparams(1),
        name="moe_dispatch",
    )(d1, d2, h2r, buf0)


def _ffn_kernel(be_ref, x_ref, wg_ref, wu_ref, wd_ref, y_ref, wgu_bf, wd_bf):
    i = pl.program_id(0)

    @pl.when((i == 0) | (be_ref[i] != be_ref[jnp.maximum(i - 1, 0)]))
    def _():
        wgu_bf[:, :D_EXPERT] = wg_ref[0].astype(BF16)
        wgu_bf[:, D_EXPERT:] = wu_ref[0].astype(BF16)
        wd_bf[...] = wd_ref[0].astype(BF16)

    acc = jnp.dot(_tiles_to_rows(x_ref[...]).astype(BF16), wgu_bf[...], preferred_element_type=F32)
    gate, up = acc[:, :D_EXPERT], acc[:, D_EXPERT:]
    hidden = (gate * jax.nn.sigmoid(gate)) * up
    y_ref[...] = _rows_to_tiles(jnp.dot(hidden.astype(BF16), wd_bf[...], preferred_element_type=F32))


def _ffn(block_expert, buf, w_gate, w_up, w_down):
    n_rows = buf.shape[0]
    n_blocks = n_rows // MOE_BLOCK
    grid_spec = pltpu.PrefetchScalarGridSpec(
        num_scalar_prefetch=1,
        grid=(n_blocks,),
        in_specs=[
            pl.BlockSpec((MOE_BLOCK, ROW_TILES, LANES), lambda i, be: (i, 0, 0)),
            pl.BlockSpec((1, D_MODEL, D_EXPERT), lambda i, be: (be[i], 0, 0)),
            pl.BlockSpec((1, D_MODEL, D_EXPERT), lambda i, be: (be[i], 0, 0)),
            pl.BlockSpec((1, D_EXPERT, D_MODEL), lambda i, be: (be[i], 0, 0)),
        ],
        out_specs=pl.BlockSpec((MOE_BLOCK, ROW_TILES, LANES), lambda i, be: (i, 0, 0)),
        scratch_shapes=[pltpu.VMEM((D_MODEL, 2 * D_EXPERT), BF16),
                        pltpu.VMEM((D_EXPERT, D_MODEL), BF16)],
    )
    return pl.pallas_call(
        _ffn_kernel,
        grid_spec=grid_spec,
        out_shape=jax.ShapeDtypeStruct((n_rows, ROW_TILES, LANES), F32),
        compiler_params=_cparams(1),
        name="moe_ffn",
    )(block_expert, buf, w_gate, w_up, w_down)


def _final_kernel(d1_ref, d2_ref, d1_next_ref, d2_next_ref, ybuf_hbm, x1_ref, gcol_ref, p_ref, wpg_ref,
                  wpp_ref, lnp_ref, lnf_ref, out_ref, y1_ref, y2_ref, sem):
    tm = x1_ref.shape[0]
    i = pl.program_id(0)
    slot = i % 2

    def gather(idx1_ref, idx2_ref, dst_slot):
        def issue(t, carry):
            pltpu.make_async_copy(ybuf_hbm.at[idx1_ref[0, 0, t]], y1_ref.at[dst_slot, t],
                                  sem.at[dst_slot]).start(priority=0)
            pltpu.make_async_copy(ybuf_hbm.at[idx2_ref[0, 0, t]], y2_ref.at[dst_slot, t],
                                  sem.at[dst_slot]).start(priority=1)
            return carry
        lax.fori_loop(0, tm, issue, 0, unroll=DMA_ISSUE_UNROLL)

    @pl.when(i == 0)
    def _():
        gather(d1_ref, d2_ref, 0)

    @pl.when(i + 1 < pl.num_programs(0))
    def _():
        gather(d1_next_ref, d2_next_ref, 1 - slot)

    pltpu.make_async_copy(ybuf_hbm.at[pl.ds(0, tm)], y1_ref.at[slot], sem.at[slot]).wait()
    pltpu.make_async_copy(ybuf_hbm.at[pl.ds(0, tm)], y2_ref.at[slot], sem.at[slot]).wait()

    g1 = gcol_ref[:, 0:1]
    g2 = gcol_ref[:, 1:2]
    x2 = x1_ref[...] + (g1 * _tiles_to_rows(y1_ref[slot]) + g2 * _tiles_to_rows(y2_ref[slot]))
    hp = _rms(x2, lnp_ref[...]).astype(BF16)
    ple_gate = jax.nn.sigmoid(jnp.dot(hp, wpg_ref[...], preferred_element_type=F32))
    ple = jnp.dot(p_ref[...].astype(BF16), wpp_ref[...], preferred_element_type=F32)
    x3 = x2 + ple_gate * ple
    out_ref[...] = _rms(x3, lnf_ref[...])


def _final(dest1, dest2, ybuf, x1, gcol, p2d, w_ple_gate, w_ple_proj, ln_ple, ln_final):
    n_tok = x1.shape[0]
    tm = min(FINAL_TM, n_tok)
    n_steps = n_tok // tm
    d1 = dest1.reshape(n_steps, 1, tm)
    d2 = dest2.reshape(n_steps, 1, tm)
    smem_spec = pl.BlockSpec((1, 1, tm), lambda i: (i, 0, 0), memory_space=pltpu.SMEM)
    smem_next = pl.BlockSpec((1, 1, tm), lambda i: (jnp.minimum(i + 1, n_steps - 1), 0, 0),
                             memory_space=pltpu.SMEM)
    return pl.pallas_call(
        _final_kernel,
        grid=(n_steps,),
        in_specs=[
            smem_spec, smem_spec, smem_next, smem_next,
            pl.BlockSpec(memory_space=pl.ANY),
            pl.BlockSpec((tm, D_MODEL), lambda i: (i, 0)),
            pl.BlockSpec((tm, LANES), lambda i: (i, 0)),
            pl.BlockSpec((tm, PLE_DIM), lambda i: (i, 0)),
            pl.BlockSpec((D_MODEL, D_MODEL), lambda i: (0, 0)),
            pl.BlockSpec((PLE_DIM, D_MODEL), lambda i: (0, 0)),
            pl.BlockSpec((1, D_MODEL), lambda i: (0, 0)),
            pl.BlockSpec((1, D_MODEL), lambda i: (0, 0)),
        ],
        out_specs=pl.BlockSpec((tm, D_MODEL), lambda i: (i, 0)),
        out_shape=jax.ShapeDtypeStruct((n_tok, D_MODEL), F32),
        scratch_shapes=[pltpu.VMEM((2, tm, ROW_TILES, LANES), F32),
                        pltpu.VMEM((2, tm, ROW_TILES, LANES), F32),
                        pltpu.SemaphoreType.DMA((2,))],
        compiler_params=_cparams(1),
        name="moe_combine_ple_final",
    )(d1, d2, d1, d2, ybuf, x1, gcol, p2d, w_ple_gate, w_ple_proj, ln_ple, ln_final)


def _layer(x2d, p2d, bsz, seq, ln_mix, w_in, conv_w, a_log, dt_bias, gdn_norm_w, sb_norm_w, w_out,
           ln_moe, w_rg, b_rg, w_re, b_re, w_eg, w_eu, w_ed, ln_ple, w_ple_gate, w_ple_proj, ln_final):
    n_tok = bsz * seq
    C, H = GDN_CHUNK, GDN_HEADS
    o_z = 3 * GDN_WIDTH
    o_b = o_z + GDN_WIDTH
    o_q = o_b + 2 * H
    w_all = jnp.concatenate(
        [w_in[:, :o_b], w_in[:, o_q:], w_in[:, o_b:o_q],
         jnp.zeros((D_MODEL, LANES - 2 * H), F32)], axis=1).astype(BF16)
    proj, ba = _in_proj(x2d, ln_mix.reshape(1, D_MODEL), w_all)
    proj3 = proj.reshape(bsz, seq, PROJ_W)
    ba_rows = (ba[:, :2 * H].reshape(bsz, seq // C, C, 2, H)
               .transpose(0, 1, 3, 4, 2).reshape(bsz, seq // C, 2, GDN_HC))
    gparams = jnp.repeat(jnp.stack([a_log, dt_bias]).astype(F32), C, axis=1)

    gdn_out = _gdn(proj3, ba_rows, conv_w, gparams, gdn_norm_w.reshape(1, GDN_HEAD_DIM))
    sb_out = _sb(proj3, jnp.tile(sb_norm_w, LANES // SB_HEAD_DIM).reshape(1, LANES))

    w_router_t = jnp.zeros((LANES, D_MODEL), F32)
    w_router_t = w_router_t.at[:N_EXPERTS].set(w_re.T).at[N_EXPERTS:N_EXPERTS + N_GROUPS].set(w_rg.T)
    b_router = jnp.zeros((LANES,), F32).at[:N_EXPERTS].set(b_re).at[N_EXPERTS:N_EXPERTS + N_GROUPS].set(b_rg)
    b_router = jnp.broadcast_to(b_router[:, None], (LANES, LANES))
    x1, h2r, route, gcol, cnt = _post_mix(
        x2d, gdn_out.reshape(n_tok, GDN_WIDTH), sb_out.reshape(n_tok, SB_WIDTH),
        w_out.astype(BF16), ln_moe.reshape(1, D_MODEL), w_router_t, b_router)

    counts = cnt[:, 0].astype(I32)
    padded = (counts + MOE_BLOCK - 1) // MOE_BLOCK * MOE_BLOCK
    padded_end = jnp.cumsum(padded)
    padded_start = padded_end - padded
    dest = _dest(route, padded_start)
    dest1, dest2 = dest[0], dest[1]
    n_blocks = (2 * n_tok + MOE_BLOCK - 1) // MOE_BLOCK + N_EXPERTS
    block_row0 = jnp.arange(n_blocks, dtype=I32) * MOE_BLOCK
    block_expert = jnp.minimum(
        jnp.sum((padded_end[:, None] <= block_row0[None, :]).astype(I32), axis=0), N_EXPERTS - 1)

    buf = _dispatch(dest1, dest2, h2r, n_blocks * MOE_BLOCK)
    ybuf = _ffn(block_expert, buf, w_eg, w_eu, w_ed)
    return _final(dest1, dest2, ybuf, x1, gcol, p2d, w_ple_gate.astype(BF16), w_ple_proj.astype(BF16),
                  ln_ple.reshape(1, D_MODEL), ln_final.reshape(1, D_MODEL))


def kernel(x, p, ln_mix, w_in, conv_w, a_log, dt_bias, gdn_norm_w, sb_norm_w, w_out, ln_moe,
           w_router_group, b_router_group, w_router_expert, b_router_expert, w_expert_gate,
           w_expert_up, w_expert_down, ln_ple, w_ple_gate, w_ple_proj, ln_final):
    bsz, seq, _ = x.shape
    depth = p.shape[0]
    assert depth == 1, "the final RMSNorm is fused into the layer's last kernel"
    x2d = x.reshape(bsz * seq, D_MODEL)
    out = _layer(x2d, p[0].reshape(bsz * seq, PLE_DIM), bsz, seq, ln_mix[0], w_in[0], conv_w[0],
                 a_log[0], dt_bias[0], gdn_norm_w[0], sb_norm_w[0], w_out[0], ln_moe[0],
                 w_router_group[0], b_router_group[0], w_router_expert[0], b_router_expert[0],
                 w_expert_gate[0], w_expert_up[0], w_expert_down[0], ln_ple[0], w_ple_gate[0],
                 w_ple_proj[0], ln_final)
    return out.reshape(bsz, seq, D_MODEL)
```

```python
import jax
import jax.numpy as jnp
from jax import lax
from jax.experimental import pallas as pl
from jax.experimental.pallas import tpu as pltpu

F32 = jnp.float32
BF16 = jnp.bfloat16
I32 = jnp.int32

D_MODEL = 1024
PLE_DIM = 256
GDN_HEADS = 4
GDN_HEAD_DIM = 128
GDN_WIDTH = GDN_HEADS * GDN_HEAD_DIM
GDN_CONV = 4
GDN_CHUNK = 64
SB_HEADS = 8
SB_HEAD_DIM = 64
SB_WIDTH = SB_HEADS * SB_HEAD_DIM
N_GROUPS = 8
EXPERTS_PER_GROUP = 8
N_EXPERTS = N_GROUPS * EXPERTS_PER_GROUP
D_EXPERT = D_MODEL // 4
MOE_BLOCK = 512
EPS = 1e-6

LANES = 128
SUBLANES = 8
ROW_TILES = D_MODEL // LANES
PROJ_W = 3 * GDN_WIDTH + GDN_WIDTH + 3 * SB_WIDTH
PROJ_CHUNK = 512
PREV_ROWS = 16
VMEM_LIMIT = 48 * 1024 * 1024

IN_TM = 512
GDN_CHUNKS_PER_STEP = 4
GDN_HC = GDN_HEADS * GDN_CHUNK
SB_TILE = 256
POST_TM = 512
DEST_TM = 2048
DISP_TM = 256
FINAL_TM = 512
DMA_ISSUE_UNROLL = 8


def _cparams(n_axes):
    return pltpu.CompilerParams(dimension_semantics=("arbitrary",) * n_axes,
                                vmem_limit_bytes=VMEM_LIMIT)


def _rms(x, w):
    return x * lax.rsqrt(jnp.mean(x * x, axis=-1, keepdims=True) + EPS) * w


def _mm(a, b):
    return jnp.dot(a.astype(BF16), b.astype(BF16), preferred_element_type=F32)


def _mm_nt(a, b):
    return lax.dot_general(a.astype(BF16), b.astype(BF16), (((1,), (1,)), ((), ())),
                           preferred_element_type=F32)


def _mm_tn(a, b):
    return lax.dot_general(a.astype(BF16), b.astype(BF16), (((0,), (0,)), ((), ())),
                           preferred_element_type=F32)


def _split(a):
    hi = a.astype(BF16)
    lo = (a - hi.astype(F32)).astype(BF16)
    return hi, lo


LOG2_E = 1.4426950408889634


def _softplus(x):
    return jnp.maximum(x, 0.0) + jnp.log(1.0 + jnp.exp2(jnp.abs(x) * (-LOG2_E)))


def _tiles_to_rows(x3):
    xt = jnp.transpose(x3, (1, 0, 2))
    return jnp.concatenate([xt[j] for j in range(ROW_TILES)], axis=1)


def _rows_to_tiles(x2):
    xt = jnp.stack([x2[:, j * LANES:(j + 1) * LANES] for j in range(ROW_TILES)], axis=0)
    return jnp.transpose(xt, (1, 0, 2))


def _in_proj_kernel(x_ref, ln_ref, w_ref, proj_ref, ba_ref):
    h = _rms(x_ref[...], ln_ref[...]).astype(BF16)
    for j in range(0, PROJ_W, PROJ_CHUNK):
        proj_ref[:, j:j + PROJ_CHUNK] = jnp.dot(
            h, w_ref[:, j:j + PROJ_CHUNK], preferred_element_type=F32).astype(BF16)
    ba_ref[...] = jnp.dot(h, w_ref[:, PROJ_W:PROJ_W + LANES], preferred_element_type=F32)


def _in_proj(x2d, ln, w_all):
    n_tok = x2d.shape[0]
    tm = min(IN_TM, n_tok)
    return pl.pallas_call(
        _in_proj_kernel,
        grid=(n_tok // tm,),
        in_specs=[pl.BlockSpec((tm, D_MODEL), lambda i: (i, 0)),
                  pl.BlockSpec((1, D_MODEL), lambda i: (0, 0)),
                  pl.BlockSpec((D_MODEL, PROJ_W + LANES), lambda i: (0, 0),
                               pipeline_mode=pl.Buffered(1))],
        out_specs=[pl.BlockSpec((tm, PROJ_W), lambda i: (i, 0)),
                   pl.BlockSpec((tm, LANES), lambda i: (i, 0))],
        out_shape=[jax.ShapeDtypeStruct((n_tok, PROJ_W), BF16),
                   jax.ShapeDtypeStruct((n_tok, LANES), F32)],
        compiler_params=_cparams(1),
        name="in_proj",
    )(x2d, ln, w_all)


def _gdn_kernel(cur_ref, prev_ref, z_ref, ba_ref, convw_ref, gp_ref, nw_ref, out_ref, state_ref):
    C, H, Dh, HC = GDN_CHUNK, GDN_HEADS, GDN_HEAD_DIM, GDN_HC
    rows_per_step = GDN_CHUNKS_PER_STEP * C
    step = pl.program_id(1)

    @pl.when(step == 0)
    def _():
        state_ref[...] = jnp.zeros_like(state_ref)

    cur = cur_ref[0].astype(F32)
    prev = jnp.where(step > 0, prev_ref[0].astype(F32), 0.0)
    stacked = jnp.concatenate([prev, cur], axis=0)
    w = convw_ref[...]
    y = w[GDN_CONV - 1:GDN_CONV, :] * cur
    for k in range(GDN_CONV - 1):
        shift = GDN_CONV - 1 - k
        y = y + w[k:k + 1, :] * pltpu.roll(stacked, shift, axis=0)[PREV_ROWS:, :]
    qkv = y * jax.nn.sigmoid(y)

    ri = lax.broadcasted_iota(I32, (HC, HC), 0)
    ci = lax.broadcasted_iota(I32, (HC, HC), 1)
    same_head = (ri // C) == (ci // C)
    incl = same_head & (ci <= ri)
    strict = same_head & (ci < ri)
    diag = ci == ri
    eye = jnp.where(diag, 1.0, 0.0).astype(F32)
    gp = gp_ref[...]

    def stack_heads(rows, base):
        return jnp.concatenate([rows[:, base + h * Dh:base + (h + 1) * Dh] for h in range(H)], axis=0)

    def scalars(n):
        ba = ba_ref[0, n]
        beta_row = jax.nn.sigmoid(ba[0:1, :])
        g_row = -jnp.exp(gp[0:1, :]) * _softplus(ba[1:2, :] + gp[1:2, :])
        gc_col = jnp.sum(jnp.where(incl, jnp.broadcast_to(g_row, (HC, HC)), 0.0), axis=1, keepdims=True)
        beta_col = jnp.sum(jnp.where(diag, jnp.broadcast_to(beta_row, (HC, HC)), 0.0), axis=1, keepdims=True)
        gc_row = jnp.sum(jnp.where(diag, jnp.broadcast_to(gc_col, (HC, HC)), 0.0), axis=0, keepdims=True)
        decay = jnp.where(incl, jnp.exp(jnp.where(incl, gc_col - gc_row, 0.0)), 0.0)
        return gc_col, beta_col, decay

    def normalized(n):
        rows = qkv[n * C:(n + 1) * C, :]
        q = stack_heads(rows, 0)
        k = stack_heads(rows, GDN_WIDTH)
        v = stack_heads(rows, 2 * GDN_WIDTH)
        q = q * lax.rsqrt(jnp.sum(q * q, axis=-1, keepdims=True) + EPS) * (Dh ** -0.5)
        k = k * lax.rsqrt(jnp.sum(k * k, axis=-1, keepdims=True) + EPS)
        return q, k, v

    chunk_ids = range(GDN_CHUNKS_PER_STEP)
    qkv_n = [normalized(n) for n in chunk_ids]
    sc = [scalars(n) for n in chunk_ids]
    kk = [_mm_nt(k, k) for _, k, _ in qkv_n]
    p = [jnp.where(strict, -(beta_col * kk_n * decay), 0.0) for (_, beta_col, decay), kk_n in zip(sc, kk)]
    t_inv = [eye + p_n for p_n in p]
    p = [_mm(p_n, p_n) for p_n in p]
    for level in range(5):
        if level < 4:
            both = [_mm(jnp.concatenate([t_n, p_n], axis=0), p_n) for t_n, p_n in zip(t_inv, p)]
            t_inv = [t_n + b_n[:HC] for t_n, b_n in zip(t_inv, both)]
            p = [b_n[HC:] for b_n in both]
        else:
            t_inv = [t_n + _mm(t_n, p_n) for t_n, p_n in zip(t_inv, p)]
    exp_gc = [jnp.exp(gc_col) for gc_col, _, _ in sc]
    sol = [_mm(t_n, jnp.concatenate([v * beta_col, k * (beta_col * e_n)], axis=1))
           for t_n, (_, k, v), (_, beta_col, _), e_n in zip(t_inv, qkv_n, sc, exp_gc)]
    qk = [_mm_nt(q, k) * decay for (q, k, _), (_, _, decay) in zip(qkv_n, sc)]
    pre = [dict(u=sol[n][:, :Dh], wk=sol[n][:, Dh:], qk=qk[n], qe=qkv_n[n][0] * exp_gc[n],
                k=qkv_n[n][1], gc=sc[n][0]) for n in chunk_ids]

    for n in chunk_ids:
        c = pre[n]
        v_new, o_state = [], []
        for h in range(H):
            rows = slice(h * C, (h + 1) * C)
            ws = _mm(jnp.concatenate([c["wk"][rows], c["qe"][rows]], axis=0), state_ref[h])
            v_new.append(c["u"][rows] - ws[:C])
            o_state.append(ws[C:])
        v_new = jnp.concatenate(v_new, axis=0)
        o = jnp.concatenate(o_state, axis=0) + _mm(c["qk"], v_new)
        for h in range(H):
            rows = slice(h * C, (h + 1) * C)
            g_last = c["gc"][(h + 1) * C - 1:(h + 1) * C, :]
            k_dec = c["k"][rows] * jnp.exp(g_last - c["gc"][rows])
            state_ref[h] = state_ref[h] * jnp.exp(g_last) + _mm_tn(k_dec, v_new[rows])
        o_n = _rms(o, nw_ref[...])
        for h in range(H):
            zz = z_ref[0, n * C:(n + 1) * C, h * Dh:(h + 1) * Dh].astype(F32)
            out_ref[0, n * C:(n + 1) * C, h * Dh:(h + 1) * Dh] = (
                o_n[h * C:(h + 1) * C] * (zz * jax.nn.sigmoid(zz))).astype(BF16)


def _gdn(proj3, ba_rows, conv_w, gparams, norm_w):
    bsz, seq, _ = proj3.shape
    cps = GDN_CHUNKS_PER_STEP
    rows = cps * GDN_CHUNK
    prev_per_step = rows // PREV_ROWS
    return pl.pallas_call(
        _gdn_kernel,
        grid=(bsz, seq // rows),
        in_specs=[
            pl.BlockSpec((1, rows, 3 * GDN_WIDTH), lambda b, s: (b, s, 0)),
            pl.BlockSpec((1, PREV_ROWS, 3 * GDN_WIDTH),
                         lambda b, s: (b, jnp.maximum(s * prev_per_step - 1, 0), 0)),
            pl.BlockSpec((1, rows, GDN_WIDTH), lambda b, s: (b, s, 3)),
            pl.BlockSpec((1, cps, 2, GDN_HC), lambda b, s: (b, s, 0, 0)),
            pl.BlockSpec((GDN_CONV, 3 * GDN_WIDTH), lambda b, s: (0, 0)),
            pl.BlockSpec((2, GDN_HC), lambda b, s: (0, 0)),
            pl.BlockSpec((1, GDN_HEAD_DIM), lambda b, s: (0, 0)),
        ],
        out_specs=pl.BlockSpec((1, rows, GDN_WIDTH), lambda b, s: (b, s, 0)),
        out_shape=jax.ShapeDtypeStruct((bsz, seq, GDN_WIDTH), BF16),
        scratch_shapes=[pltpu.VMEM((GDN_HEADS, GDN_HEAD_DIM, GDN_HEAD_DIM), F32)],
        compiler_params=_cparams(2),
        name="gdn",
    )(proj3, proj3, proj3, ba_rows, conv_w, gparams, norm_w)


def _sb_kernel(q_ref, k_ref, v_ref, nw_ref, out_ref, acc_ref, later_ref):
    T, Dh = SB_TILE, SB_HEAD_DIM
    qb = pl.program_id(2)
    row = lax.broadcasted_iota(I32, (T, T), 0)
    col = lax.broadcasted_iota(I32, (T, T), 1)
    causal = col < row
    neg_suffix = jnp.where(row >= col, -1.0, 0.0).astype(BF16)
    neg_suffix2 = jnp.concatenate([neg_suffix, neg_suffix], axis=0)

    low = lax.broadcasted_iota(I32, (T, LANES), 1) < Dh
    q = q_ref[0] * jnp.asarray(Dh ** -0.5, BF16)
    zeros = jnp.zeros_like(q)
    q_heads = (jnp.where(low, q, zeros), jnp.where(low, zeros, q))

    def tiles(kbs, first_is_diagonal):
        blocks = []
        for n, kb in enumerate(kbs):
            start = pl.multiple_of(kb * T, T)
            blocks.append((k_ref[0, pl.ds(start, T), :], v_ref[0, pl.ds(start, T), :],
                           first_is_diagonal and n == 0))
        chains = [(i, k_blk, v_blk, on_diag) for i in range(2) for k_blk, v_blk, on_diag in blocks]
        zs = [lax.dot_general(q_heads[i], k_blk, (((1,), (1,)), ((), ())), preferred_element_type=F32)
              for i, k_blk, _, _ in chains]
        splits = []
        for (_, _, _, on_diag), z in zip(chains, zs):
            neg_log_remain = _softplus(z)
            if on_diag:
                neg_log_remain = jnp.where(causal, neg_log_remain, 0.0)
            splits.append(neg_log_remain.astype(BF16))
        incl_all = jnp.dot(jnp.concatenate(splits, axis=0), neg_suffix, preferred_element_type=F32)
        incl_after = [incl_all[n * T:(n + 1) * T] for n in range(len(chains))]
        later = [later_ref[0], later_ref[1]]
        weights = []
        for (i, _, _, on_diag), z, incl in zip(chains, zs, incl_after):
            w = jnp.exp(z + incl + later[i])
            if on_diag:
                w = jnp.where(causal, w, 0.0)
            weights.append(w.astype(BF16))
            later[i] = later[i] + incl[:, 0:1]
        acc = [acc_ref[0], acc_ref[1]]
        for (i, _, v_blk, _), w in zip(chains, weights):
            acc[i] = acc[i] + jnp.dot(w, v_blk, preferred_element_type=F32)
        for i in range(2):
            later_ref[i] = later[i]
            acc_ref[i] = acc[i]

    later_ref[...] = jnp.zeros_like(later_ref)
    acc_ref[...] = jnp.zeros_like(acc_ref)

    head = (qb % 4) + 1

    @pl.when(head == 1)
    def _():
        tiles([qb], True)

    @pl.when(head == 2)
    def _():
        tiles([qb, qb - 1], True)

    @pl.when(head == 3)
    def _():
        tiles([qb], True)
        tiles([qb - 1, qb - 2], False)

    @pl.when(head == 4)
    def _():
        tiles([qb, qb - 1, qb - 2, qb - 3], True)

    def quad(i, carry):
        kb = qb - head - 4 * i
        tiles([kb, kb - 1, kb - 2, kb - 3], False)
        return carry

    lax.fori_loop(0, (qb + 1 - head) // 4, quad, 0)

    o = jnp.where(low, acc_ref[0], acc_ref[1])
    hr = lax.broadcasted_iota(I32, (LANES, LANES), 0) // Dh
    hc = lax.broadcasted_iota(I32, (LANES, LANES), 1) // Dh
    avg = jnp.where(hr == hc, 1.0 / Dh, 0.0).astype(BF16)
    hi, lo = _split(o * o)
    ms = jnp.dot(jnp.concatenate([hi, lo], axis=1), jnp.concatenate([avg, avg], axis=0),
                 preferred_element_type=F32)
    out_ref[0] = (o * lax.rsqrt(ms + EPS) * nw_ref[...]).astype(BF16)


def _sb(proj3, norm_w2):
    bsz, seq, _ = proj3.shape
    tile = min(SB_TILE, seq)
    assert tile == SB_TILE and seq % tile == 0
    pairs = SB_WIDTH // LANES
    q_col = (3 * GDN_WIDTH + GDN_WIDTH) // LANES
    k_col = q_col + pairs
    v_col = k_col + pairs
    return pl.pallas_call(
        _sb_kernel,
        grid=(bsz, pairs, seq // tile),
        in_specs=[
            pl.BlockSpec((1, tile, LANES), lambda b, hp, qb: (b, qb, q_col + hp)),
            pl.BlockSpec((1, seq, LANES), lambda b, hp, qb: (b, 0, k_col + hp)),
            pl.BlockSpec((1, seq, LANES), lambda b, hp, qb: (b, 0, v_col + hp)),
            pl.BlockSpec((1, LANES), lambda b, hp, qb: (0, 0)),
        ],
        out_specs=pl.BlockSpec((1, tile, LANES), lambda b, hp, qb: (b, qb, hp)),
        out_shape=jax.ShapeDtypeStruct((bsz, seq, SB_WIDTH), BF16),
        scratch_shapes=[pltpu.VMEM((2, tile, LANES), F32),
                        pltpu.VMEM((2, tile, 1), F32)],
        compiler_params=_cparams(3),
        name="sb_attn",
    )(proj3, proj3, proj3, norm_w2)


def _post_mix_kernel(x_ref, g_ref, s_ref, wo_ref, ln_ref, wrt_ref, br_ref,
                     x1_ref, h2r_ref, route_ref, gcol_ref, cnt_ref, count_ref):
    tm = x_ref.shape[0]
    i = pl.program_id(0)

    @pl.when(i == 0)
    def _():
        count_ref[...] = jnp.zeros_like(count_ref)

    x1 = (x_ref[...]
          + jnp.dot(g_ref[...], wo_ref[0:GDN_WIDTH, :], preferred_element_type=F32)
          + jnp.dot(s_ref[...], wo_ref[GDN_WIDTH:, :], preferred_element_type=F32))
    x1_ref[...] = x1
    h2 = _rms(x1, ln_ref[...])
    h2r_ref[...] = _rows_to_tiles(h2)

    w_hi, w_lo = _split(wrt_ref[...])
    h_hi, h_lo = _split(h2)
    nt = (((1,), (1,)), ((), ()))
    logits = (lax.dot_general(w_hi, h_hi, nt, preferred_element_type=F32)
              + lax.dot_general(w_hi, h_lo, nt, preferred_element_type=F32)
              + lax.dot_general(w_lo, h_hi, nt, preferred_element_type=F32)) + br_ref[:, 0:1]

    g_logits = logits[N_EXPERTS:N_EXPERTS + N_GROUPS, :]
    g_max = jnp.max(g_logits, axis=0, keepdims=True)
    group_w = 1.0 / jnp.sum(jnp.exp(g_logits - g_max), axis=0, keepdims=True)
    sub8 = lax.broadcasted_iota(I32, (N_GROUPS, tm), 0)
    g_idx = jnp.min(jnp.where(g_logits == g_max, sub8, N_GROUPS), axis=0, keepdims=True)

    e_logits = logits[0:N_EXPERTS, :]
    erow = lax.broadcasted_iota(I32, (N_EXPERTS, tm), 0)
    neg_inf = -jnp.inf
    in_group = jnp.where((erow // EXPERTS_PER_GROUP) == g_idx, e_logits, neg_inf)
    m1 = jnp.max(in_group, axis=0, keepdims=True)
    idx1 = jnp.min(jnp.where(in_group == m1, erow, N_EXPERTS), axis=0, keepdims=True)
    rest = jnp.where(erow == idx1, neg_inf, in_group)
    m2 = jnp.max(rest, axis=0, keepdims=True)
    idx2 = jnp.min(jnp.where(rest == m2, erow, N_EXPERTS), axis=0, keepdims=True)
    r = jnp.exp(m2 - m1)
    w1 = 1.0 / (1.0 + r)
    gate1 = group_w * w1
    gate2 = group_w * (r * w1)

    oh1 = erow == idx1
    oh2 = erow == idx2
    picked = jnp.where(oh1 | oh2, 1.0, 0.0)
    tr = lax.broadcasted_iota(I32, (tm, tm), 0)
    tc = lax.broadcasted_iota(I32, (tm, tm), 1)
    before = jnp.where(tr < tc, 1.0, 0.0).astype(BF16)
    seen = jnp.dot(picked.astype(BF16), before, preferred_element_type=F32) + count_ref[:, 0:1]
    rank1 = jnp.sum(jnp.where(oh1, seen, 0.0), axis=0, keepdims=True)
    rank2 = jnp.sum(jnp.where(oh2, seen, 0.0), axis=0, keepdims=True)
    count_ref[...] = count_ref[...] + jnp.sum(picked, axis=1, keepdims=True)
    cnt_ref[...] = count_ref[...]

    sub = lax.broadcasted_iota(I32, (SUBLANES, tm), 0)
    rows = (idx1.astype(F32), idx2.astype(F32), rank1, rank2)
    route = jnp.zeros((SUBLANES, tm), F32)
    for n, val in enumerate(rows):
        route = jnp.where(sub == n, val, route)
    route_ref[...] = route

    sub_l = lax.broadcasted_iota(I32, (LANES, tm), 0)
    gates_t = jnp.where(sub_l == 0, gate1, jnp.where(sub_l == 1, gate2, 0.0))
    gcol_ref[...] = gates_t.T


def _post_mix(x2d, gdn_out, sb_out, w_out, ln, w_router_t, b_router):
    n_tok = x2d.shape[0]
    tm = min(POST_TM, n_tok)
    return pl.pallas_call(
        _post_mix_kernel,
        grid=(n_tok // tm,),
        in_specs=[
            pl.BlockSpec((tm, D_MODEL), lambda i: (i, 0)),
            pl.BlockSpec((tm, GDN_WIDTH), lambda i: (i, 0)),
            pl.BlockSpec((tm, SB_WIDTH), lambda i: (i, 0)),
            pl.BlockSpec((D_MODEL, D_MODEL), lambda i: (0, 0)),
            pl.BlockSpec((1, D_MODEL), lambda i: (0, 0)),
            pl.BlockSpec((LANES, D_MODEL), lambda i: (0, 0)),
            pl.BlockSpec((LANES, LANES), lambda i: (0, 0)),
        ],
        out_specs=[
            pl.BlockSpec((tm, D_MODEL), lambda i: (i, 0)),
            pl.BlockSpec((tm, ROW_TILES, LANES), lambda i: (i, 0, 0)),
            pl.BlockSpec((SUBLANES, tm), lambda i: (0, i)),
            pl.BlockSpec((tm, LANES), lambda i: (i, 0)),
            pl.BlockSpec((N_EXPERTS, LANES), lambda i: (0, 0)),
        ],
        out_shape=[
            jax.ShapeDtypeStruct((n_tok, D_MODEL), F32),
            jax.ShapeDtypeStruct((n_tok, ROW_TILES, LANES), F32),
            jax.ShapeDtypeStruct((SUBLANES, n_tok), F32),
            jax.ShapeDtypeStruct((n_tok, LANES), F32),
            jax.ShapeDtypeStruct((N_EXPERTS, LANES), F32),
        ],
        scratch_shapes=[pltpu.VMEM((N_EXPERTS, LANES), F32)],
        compiler_params=_cparams(1),
        name="post_mix",
    )(x2d, gdn_out, sb_out, w_out, ln, w_router_t, b_router)


def _dest_kernel(route_ref, start_ref, dest_ref):
    tm = route_ref.shape[1]
    erow = lax.broadcasted_iota(I32, (N_EXPERTS, tm), 0).astype(F32)
    starts = start_ref[:, 0:1]
    sub = lax.broadcasted_iota(I32, (SUBLANES, tm), 0)
    dest = jnp.zeros((SUBLANES, tm), F32)
    for n in range(2):
        expert = route_ref[n:n + 1, :]
        rank = route_ref[2 + n:3 + n, :]
        row = jnp.sum(jnp.where(erow == expert, starts, 0.0), axis=0, keepdims=True) + rank
        dest = jnp.where(sub == n, row, dest)
    dest_ref[...] = dest.astype(I32)


def _dest(route, padded_start):
    n_tok = route.shape[1]
    tm = min(DEST_TM, n_tok)
    starts = jnp.broadcast_to(padded_start.astype(F32)[:, None], (N_EXPERTS, LANES))
    return pl.pallas_call(
        _dest_kernel,
        grid=(n_tok // tm,),
        in_specs=[pl.BlockSpec((SUBLANES, tm), lambda i: (0, i)),
                  pl.BlockSpec((N_EXPERTS, LANES), lambda i: (0, 0))],
        out_specs=pl.BlockSpec((SUBLANES, tm), lambda i: (0, i)),
        out_shape=jax.ShapeDtypeStruct((SUBLANES, n_tok), I32),
        compiler_params=_cparams(1),
        name="moe_dest",
    )(route, starts)


def _dispatch_kernel(d1_ref, d2_ref, rows_ref, buf_in_hbm, buf_hbm, sem):
    del buf_in_hbm
    tm = rows_ref.shape[0]

    def issue(t, carry):
        pltpu.make_async_copy(rows_ref.at[t], buf_hbm.at[d1_ref[0, 0, t]], sem).start(priority=0)
        pltpu.make_async_copy(rows_ref.at[t], buf_hbm.at[d2_ref[0, 0, t]], sem).start(priority=1)
        return carry

    lax.fori_loop(0, tm, issue, 0, unroll=DMA_ISSUE_UNROLL)
    for _ in range(2):
        pltpu.make_async_copy(rows_ref, buf_hbm.at[pl.ds(0, tm)], sem).wait()


def _dispatch(dest1, dest2, h2r, n_rows):
    n_tok = h2r.shape[0]
    tm = min(DISP_TM, n_tok)
    n_steps = n_tok // tm
    d1 = dest1.reshape(n_steps, 1, tm)
    d2 = dest2.reshape(n_steps, 1, tm)
    smem_spec = pl.BlockSpec((1, 1, tm), lambda i: (i, 0, 0), memory_space=pltpu.SMEM)
    buf0 = jnp.zeros((n_rows, ROW_TILES, LANES), F32)
    return pl.pallas_call(
        _dispatch_kernel,
        grid=(n_steps,),
        in_specs=[smem_spec, smem_spec,
                  pl.BlockSpec((tm, ROW_TILES, LANES), lambda i: (i, 0, 0)),
                  pl.BlockSpec(memory_space=pl.ANY)],
        out_specs=pl.BlockSpec(memory_space=pl.ANY),
        out_shape=jax.ShapeDtypeStruct((n_rows, ROW_TILES, LANES), F32),
        scratch_shapes=[pltpu.SemaphoreType.DMA(())],
        input_output_aliases={3: 0},
        compiler_params=_cparams(1),
        name="moe_dispatch",
    )(d1, d2, h2r, buf0)


def _ffn_kernel(be_ref, x_ref, wg_ref, wu_ref, wd_ref, y_ref, wgu_bf, wd_bf):
    i = pl.program_id(0)

    @pl.when((i == 0) | (be_ref[i] != be_ref[jnp.maximum(i - 1, 0)]))
    def _():
        wgu_bf[:, :D_EXPERT] = wg_ref[0].astype(BF16)
        wgu_bf[:, D_EXPERT:] = wu_ref[0].astype(BF16)
        wd_bf[...] = wd_ref[0].astype(BF16)

    acc = jnp.dot(_tiles_to_rows(x_ref[...]).astype(BF16), wgu_bf[...], preferred_element_type=F32)
    gate, up = acc[:, :D_EXPERT], acc[:, D_EXPERT:]
    hidden = (gate * jax.nn.sigmoid(gate)) * up
    y_ref[...] = _rows_to_tiles(jnp.dot(hidden.astype(BF16), wd_bf[...], preferred_element_type=F32))


def _ffn(block_expert, buf, w_gate, w_up, w_down):
    n_rows = buf.shape[0]
    n_blocks = n_rows // MOE_BLOCK
    grid_spec = pltpu.PrefetchScalarGridSpec(
        num_scalar_prefetch=1,
        grid=(n_blocks,),
        in_specs=[
            pl.BlockSpec((MOE_BLOCK, ROW_TILES, LANES), lambda i, be: (i, 0, 0)),
            pl.BlockSpec((1, D_MODEL, D_EXPERT), lambda i, be: (be[i], 0, 0)),
            pl.BlockSpec((1, D_MODEL, D_EXPERT), lambda i, be: (be[i], 0, 0)),
            pl.BlockSpec((1, D_EXPERT, D_MODEL), lambda i, be: (be[i], 0, 0)),
        ],
        out_specs=pl.BlockSpec((MOE_BLOCK, ROW_TILES, LANES), lambda i, be: (i, 0, 0)),
        scratch_shapes=[pltpu.VMEM((D_MODEL, 2 * D_EXPERT), BF16),
                        pltpu.VMEM((D_EXPERT, D_MODEL), BF16)],
    )
    return pl.pallas_call(
        _ffn_kernel,
        grid_spec=grid_spec,
        out_shape=jax.ShapeDtypeStruct((n_rows, ROW_TILES, LANES), F32),
        compiler_params=_cparams(1),
        name="moe_ffn",
    )(block_expert, buf, w_gate, w_up, w_down)


def _final_kernel(d1_ref, d2_ref, d1_next_ref, d2_next_ref, ybuf_hbm, x1_ref, gcol_ref, p_ref, wpg_ref,
                  wpp_ref, lnp_ref, lnf_ref, out_ref, y1_ref, y2_ref, sem):
    tm = x1_ref.shape[0]
    i = pl.program_id(0)
    slot = i % 2

    def gather(idx1_ref, idx2_ref, dst_slot):
        def issue(t, carry):
            pltpu.make_async_copy(ybuf_hbm.at[idx1_ref[0, 0, t]], y1_ref.at[dst_slot, t],
                                  sem.at[dst_slot]).start(priority=0)
            pltpu.make_async_copy(ybuf_hbm.at[idx2_ref[0, 0, t]], y2_ref.at[dst_slot, t],
                                  sem.at[dst_slot]).start(priority=1)
            return carry
        lax.fori_loop(0, tm, issue, 0, unroll=DMA_ISSUE_UNROLL)

    @pl.when(i == 0)
    def _():
        gather(d1_ref, d2_ref, 0)

    @pl.when(i + 1 < pl.num_programs(0))
    def _():
        gather(d1_next_ref, d2_next_ref, 1 - slot)

    pltpu.make_async_copy(ybuf_hbm.at[pl.ds(0, tm)], y1_ref.at[slot], sem.at[slot]).wait()
    pltpu.make_async_copy(ybuf_hbm.at[pl.ds(0, tm)], y2_ref.at[slot], sem.at[slot]).wait()

    g1 = gcol_ref[:, 0:1]
    g2 = gcol_ref[:, 1:2]
    x2 = x1_ref[...] + (g1 * _tiles_to_rows(y1_ref[slot]) + g2 * _tiles_to_rows(y2_ref[slot]))
    hp = _rms(x2, lnp_ref[...]).astype(BF16)
    ple_gate = jax.nn.sigmoid(jnp.dot(hp, wpg_ref[...], preferred_element_type=F32))
    ple = jnp.dot(p_ref[...].astype(BF16), wpp_ref[...], preferred_element_type=F32)
    x3 = x2 + ple_gate * ple
    out_ref[...] = _rms(x3, lnf_ref[...])


def _final(dest1, dest2, ybuf, x1, gcol, p2d, w_ple_gate, w_ple_proj, ln_ple, ln_final):
    n_tok = x1.shape[0]
    tm = min(FINAL_TM, n_tok)
    n_steps = n_tok // tm
    d1 = dest1.reshape(n_steps, 1, tm)
    d2 = dest2.reshape(n_steps, 1, tm)
    smem_spec = pl.BlockSpec((1, 1, tm), lambda i: (i, 0, 0), memory_space=pltpu.SMEM)
    smem_next = pl.BlockSpec((1, 1, tm), lambda i: (jnp.minimum(i + 1, n_steps - 1), 0, 0),
                             memory_space=pltpu.SMEM)
    return pl.pallas_call(
        _final_kernel,
        grid=(n_steps,),
        in_specs=[
            smem_spec, smem_spec, smem_next, smem_next,
            pl.BlockSpec(memory_space=pl.ANY),
            pl.BlockSpec((tm, D_MODEL), lambda i: (i, 0)),
            pl.BlockSpec((tm, LANES), lambda i: (i, 0)),
            pl.BlockSpec((tm, PLE_DIM), lambda i: (i, 0)),
            pl.BlockSpec((D_MODEL, D_MODEL), lambda i: (0, 0)),
            pl.BlockSpec((PLE_DIM, D_MODEL), lambda i: (0, 0)),
            pl.BlockSpec((1, D_MODEL), lambda i: (0, 0)),
            pl.BlockSpec((1, D_MODEL), lambda i: (0, 0)),
        ],
        out_specs=pl.BlockSpec((tm, D_MODEL), lambda i: (i, 0)),
        out_shape=jax.ShapeDtypeStruct((n_tok, D_MODEL), F32),
        scratch_shapes=[pltpu.VMEM((2, tm, ROW_TILES, LANES), F32),
                        pltpu.VMEM((2, tm, ROW_TILES, LANES), F32),
                        pltpu.SemaphoreType.DMA((2,))],
        compiler_params=_cparams(1),
        name="moe_combine_ple_final",
    )(d1, d2, d1, d2, ybuf, x1, gcol, p2d, w_ple_gate, w_ple_proj, ln_ple, ln_final)


def _layer(x2d, p2d, bsz, seq, ln_mix, w_in, conv_w, a_log, dt_bias, gdn_norm_w, sb_norm_w, w_out,
           ln_moe, w_rg, b_rg, w_re, b_re, w_eg, w_eu, w_ed, ln_ple, w_ple_gate, w_ple_proj, ln_final):
    n_tok = bsz * seq
    C, H = GDN_CHUNK, GDN_HEADS
    o_z = 3 * GDN_WIDTH
    o_b = o_z + GDN_WIDTH
    o_q = o_b + 2 * H
    w_all = jnp.concatenate(
        [w_in[:, :o_b], w_in[:, o_q:], w_in[:, o_b:o_q],
         jnp.zeros((D_MODEL, LANES - 2 * H), F32)], axis=1).astype(BF16)
    proj, ba = _in_proj(x2d, ln_mix.reshape(1, D_MODEL), w_all)
    proj3 = proj.reshape(bsz, seq, PROJ_W)
    ba_rows = (ba[:, :2 * H].reshape(bsz, seq // C, C, 2, H)
               .transpose(0, 1, 3, 4, 2).reshape(bsz, seq // C, 2, GDN_HC))
    gparams = jnp.repeat(jnp.stack([a_log, dt_bias]).astype(F32), C, axis=1)

    gdn_out = _gdn(proj3, ba_rows, conv_w, gparams, gdn_norm_w.reshape(1, GDN_HEAD_DIM))
    sb_out = _sb(proj3, jnp.tile(sb_norm_w, LANES // SB_HEAD_DIM).reshape(1, LANES))

    w_router_t = jnp.zeros((LANES, D_MODEL), F32)
    w_router_t = w_router_t.at[:N_EXPERTS].set(w_re.T).at[N_EXPERTS:N_EXPERTS + N_GROUPS].set(w_rg.T)
    b_router = jnp.zeros((LANES,), F32).at[:N_EXPERTS].set(b_re).at[N_EXPERTS:N_EXPERTS + N_GROUPS].set(b_rg)
    b_router = jnp.broadcast_to(b_router[:, None], (LANES, LANES))
    x1, h2r, route, gcol, cnt = _post_mix(
        x2d, gdn_out.reshape(n_tok, GDN_WIDTH), sb_out.reshape(n_tok, SB_WIDTH),
        w_out.astype(BF16), ln_moe.reshape(1, D_MODEL), w_router_t, b_router)

    counts = cnt[:, 0].astype(I32)
    padded = (counts + MOE_BLOCK - 1) // MOE_BLOCK * MOE_BLOCK
    padded_end = jnp.cumsum(padded)
    padded_start = padded_end - padded
    dest = _dest(route, padded_start)
    dest1, dest2 = dest[0], dest[1]
    n_blocks = (2 * n_tok + MOE_BLOCK - 1) // MOE_BLOCK + N_EXPERTS
    block_row0 = jnp.arange(n_blocks, dtype=I32) * MOE_BLOCK
    block_expert = jnp.minimum(
        jnp.sum((padded_end[:, None] <= block_row0[None, :]).astype(I32), axis=0), N_EXPERTS - 1)

    buf = _dispatch(dest1, dest2, h2r, n_blocks * MOE_BLOCK)
    ybuf = _ffn(block_expert, buf, w_eg, w_eu, w_ed)
    return _final(dest1, dest2, ybuf, x1, gcol, p2d, w_ple_gate.astype(BF16), w_ple_proj.astype(BF16),
                  ln_ple.reshape(1, D_MODEL), ln_final.reshape(1, D_MODEL))


def kernel(x, p, ln_mix, w_in, conv_w, a_log, dt_bias, gdn_norm_w, sb_norm_w, w_out, ln_moe,
           w_router_group, b_router_group, w_router_expert, b_router_expert, w_expert_gate,
           w_expert_up, w_expert_down, ln_ple, w_ple_gate, w_ple_proj, ln_final):
    bsz, seq, _ = x.shape
    depth = p.shape[0]
    assert depth == 1, "the final RMSNorm is fused into the layer's last kernel"
    x2d = x.reshape(bsz * seq, D_MODEL)
    out = _layer(x2d, p[0].reshape(bsz * seq, PLE_DIM), bsz, seq, ln_mix[0], w_in[0], conv_w[0],
                 a_log[0], dt_bias[0], gdn_norm_w[0], sb_norm_w[0], w_out[0], ln_moe[0],
                 w_router_group[0], b_router_group[0], w_router_expert[0], b_router_expert[0],
                 w_expert_gate[0], w_expert_up[0], w_expert_down[0], ln_ple[0], w_ple_gate[0],
                 w_ple_proj[0], ln_final)
    return out.reshape(bsz, seq, D_MODEL)
```

```python
import jax
import jax.numpy as jnp
from jax import lax
from jax.experimental import pallas as pl
from jax.experimental.pallas import tpu as pltpu

F32 = jnp.float32
BF16 = jnp.bfloat16
I32 = jnp.int32

D_MODEL = 1024
PLE_DIM = 256
GDN_HEADS = 4
GDN_HEAD_DIM = 128
GDN_WIDTH = GDN_HEADS * GDN_HEAD_DIM
GDN_CONV = 4
GDN_CHUNK = 64
SB_HEADS = 8
SB_HEAD_DIM = 64
SB_WIDTH = SB_HEADS * SB_HEAD_DIM
N_GROUPS = 8
EXPERTS_PER_GROUP = 8
N_EXPERTS = N_GROUPS * EXPERTS_PER_GROUP
D_EXPERT = D_MODEL // 4
MOE_BLOCK = 512
EPS = 1e-6

LANES = 128
SUBLANES = 8
ROW_TILES = D_MODEL // LANES
PROJ_W = 3 * GDN_WIDTH + GDN_WIDTH + 3 * SB_WIDTH
PROJ_CHUNK = 512
PREV_ROWS = 16
VMEM_LIMIT = 48 * 1024 * 1024

IN_TM = 512
GDN_CHUNKS_PER_STEP = 8
GDN_HC = GDN_HEADS * GDN_CHUNK
SB_TILE = 256
POST_TM = 512
DEST_TM = 2048
DISP_TM = 256
FINAL_TM = 512
DMA_ISSUE_UNROLL = 8


def _cparams(n_axes):
    return pltpu.CompilerParams(dimension_semantics=("arbitrary",) * n_axes,
                                vmem_limit_bytes=VMEM_LIMIT)


def _rms(x, w):
    return x * lax.rsqrt(jnp.mean(x * x, axis=-1, keepdims=True) + EPS) * w


def _mm(a, b):
    return jnp.dot(a.astype(BF16), b.astype(BF16), preferred_element_type=F32)


def _mm_nt(a, b):
    return lax.dot_general(a.astype(BF16), b.astype(BF16), (((1,), (1,)), ((), ())),
                           preferred_element_type=F32)


def _mm_tn(a, b):
    return lax.dot_general(a.astype(BF16), b.astype(BF16), (((0,), (0,)), ((), ())),
                           preferred_element_type=F32)


def _split(a):
    hi = a.astype(BF16)
    lo = (a - hi.astype(F32)).astype(BF16)
    return hi, lo


LOG2_E = 1.4426950408889634


def _softplus(x):
    return jnp.maximum(x, 0.0) + jnp.log(1.0 + jnp.exp2(jnp.abs(x) * (-LOG2_E)))


def _tiles_to_rows(x3):
    xt = jnp.transpose(x3, (1, 0, 2))
    return jnp.concatenate([xt[j] for j in range(ROW_TILES)], axis=1)


def _rows_to_tiles(x2):
    xt = jnp.stack([x2[:, j * LANES:(j + 1) * LANES] for j in range(ROW_TILES)], axis=0)
    return jnp.transpose(xt, (1, 0, 2))


def _in_proj_kernel(x_ref, ln_ref, w_ref, proj_ref, ba_ref):
    h = _rms(x_ref[...], ln_ref[...]).astype(BF16)
    for j in range(0, PROJ_W, PROJ_CHUNK):
        proj_ref[:, j:j + PROJ_CHUNK] = jnp.dot(
            h, w_ref[:, j:j + PROJ_CHUNK], preferred_element_type=F32).astype(BF16)
    ba_ref[...] = jnp.dot(h, w_ref[:, PROJ_W:PROJ_W + LANES], preferred_element_type=F32)


def _in_proj(x2d, ln, w_all):
    n_tok = x2d.shape[0]
    tm = min(IN_TM, n_tok)
    return pl.pallas_call(
        _in_proj_kernel,
        grid=(n_tok // tm,),
        in_specs=[pl.BlockSpec((tm, D_MODEL), lambda i: (i, 0)),
                  pl.BlockSpec((1, D_MODEL), lambda i: (0, 0)),
                  pl.BlockSpec((D_MODEL, PROJ_W + LANES), lambda i: (0, 0),
                               pipeline_mode=pl.Buffered(1))],
        out_specs=[pl.BlockSpec((tm, PROJ_W), lambda i: (i, 0)),
                   pl.BlockSpec((tm, LANES), lambda i: (i, 0))],
        out_shape=[jax.ShapeDtypeStruct((n_tok, PROJ_W), BF16),
                   jax.ShapeDtypeStruct((n_tok, LANES), F32)],
        compiler_params=_cparams(1),
        name="in_proj",
    )(x2d, ln, w_all)


def _gdn_kernel(cur_ref, prev_ref, z_ref, ba_ref, convw_ref, gp_ref, nw_ref, out_ref, state_ref):
    C, H, Dh, HC = GDN_CHUNK, GDN_HEADS, GDN_HEAD_DIM, GDN_HC
    rows_per_step = GDN_CHUNKS_PER_STEP * C
    step = pl.program_id(1)

    @pl.when(step == 0)
    def _():
        state_ref[...] = jnp.zeros_like(state_ref)

    cur = cur_ref[0].astype(F32)
    prev = jnp.where(step > 0, prev_ref[0].astype(F32), 0.0)
    stacked = jnp.concatenate([prev, cur], axis=0)
    w = convw_ref[...]
    y = w[GDN_CONV - 1:GDN_CONV, :] * cur
    for k in range(GDN_CONV - 1):
        shift = GDN_CONV - 1 - k
        y = y + w[k:k + 1, :] * pltpu.roll(stacked, shift, axis=0)[PREV_ROWS:, :]
    qkv = y * jax.nn.sigmoid(y)

    ri = lax.broadcasted_iota(I32, (HC, HC), 0)
    ci = lax.broadcasted_iota(I32, (HC, HC), 1)
    same_head = (ri // C) == (ci // C)
    incl = same_head & (ci <= ri)
    strict = same_head & (ci < ri)
    diag = ci == ri
    eye = jnp.where(diag, 1.0, 0.0).astype(F32)
    gp = gp_ref[...]

    def stack_heads(rows, base):
        return jnp.concatenate([rows[:, base + h * Dh:base + (h + 1) * Dh] for h in range(H)], axis=0)

    def scalars(n):
        ba = ba_ref[0, n]
        beta_row = jax.nn.sigmoid(ba[0:1, :])
        g_row = -jnp.exp(gp[0:1, :]) * _softplus(ba[1:2, :] + gp[1:2, :])
        gc_col = jnp.sum(jnp.where(incl, jnp.broadcast_to(g_row, (HC, HC)), 0.0), axis=1, keepdims=True)
        beta_col = jnp.sum(jnp.where(diag, jnp.broadcast_to(beta_row, (HC, HC)), 0.0), axis=1, keepdims=True)
        gc_row = jnp.sum(jnp.where(diag, jnp.broadcast_to(gc_col, (HC, HC)), 0.0), axis=0, keepdims=True)
        decay = jnp.where(incl, jnp.exp(jnp.where(incl, gc_col - gc_row, 0.0)), 0.0)
        return gc_col, beta_col, decay

    def normalized(n):
        rows = qkv[n * C:(n + 1) * C, :]
        q = stack_heads(rows, 0)
        k = stack_heads(rows, GDN_WIDTH)
        v = stack_heads(rows, 2 * GDN_WIDTH)
        q = q * lax.rsqrt(jnp.sum(q * q, axis=-1, keepdims=True) + EPS) * (Dh ** -0.5)
        k = k * lax.rsqrt(jnp.sum(k * k, axis=-1, keepdims=True) + EPS)
        return q, k, v

    chunk_ids = range(GDN_CHUNKS_PER_STEP)
    qkv_n = [normalized(n) for n in chunk_ids]
    sc = [scalars(n) for n in chunk_ids]
    kk = [_mm_nt(k, k) for _, k, _ in qkv_n]
    p = [jnp.where(strict, -(beta_col * kk_n * decay), 0.0) for (_, beta_col, decay), kk_n in zip(sc, kk)]
    t_inv = [eye + p_n for p_n in p]
    p = [_mm(p_n, p_n) for p_n in p]
    for level in range(5):
        if level < 4:
            both = [_mm(jnp.concatenate([t_n, p_n], axis=0), p_n) for t_n, p_n in zip(t_inv, p)]
            t_inv = [t_n + b_n[:HC] for t_n, b_n in zip(t_inv, both)]
            p = [b_n[HC:] for b_n in both]
        else:
            t_inv = [t_n + _mm(t_n, p_n) for t_n, p_n in zip(t_inv, p)]
    exp_gc = [jnp.exp(gc_col) for gc_col, _, _ in sc]
    sol = [_mm(t_n, jnp.concatenate([v * beta_col, k * (beta_col * e_n)], axis=1))
           for t_n, (_, k, v), (_, beta_col, _), e_n in zip(t_inv, qkv_n, sc, exp_gc)]
    qk = [_mm_nt(q, k) * decay for (q, k, _), (_, _, decay) in zip(qkv_n, sc)]
    pre = [dict(u=sol[n][:, :Dh], wk=sol[n][:, Dh:], qk=qk[n], qe=qkv_n[n][0] * exp_gc[n],
                k=qkv_n[n][1], gc=sc[n][0]) for n in chunk_ids]

    for n in chunk_ids:
        c = pre[n]
        v_new, o_state = [], []
        for h in range(H):
            rows = slice(h * C, (h + 1) * C)
            ws = _mm(jnp.concatenate([c["wk"][rows], c["qe"][rows]], axis=0), state_ref[h])
            v_new.append(c["u"][rows] - ws[:C])
            o_state.append(ws[C:])
        v_new = jnp.concatenate(v_new, axis=0)
        o = jnp.concatenate(o_state, axis=0) + _mm(c["qk"], v_new)
        for h in range(H):
            rows = slice(h * C, (h + 1) * C)
            g_last = c["gc"][(h + 1) * C - 1:(h + 1) * C, :]
            k_dec = c["k"][rows] * jnp.exp(g_last - c["gc"][rows])
            state_ref[h] = state_ref[h] * jnp.exp(g_last) + _mm_tn(k_dec, v_new[rows])
        o_n = _rms(o, nw_ref[...])
        for h in range(H):
            zz = z_ref[0, n * C:(n + 1) * C, h * Dh:(h + 1) * Dh].astype(F32)
            out_ref[0, n * C:(n + 1) * C, h * Dh:(h + 1) * Dh] = (
                o_n[h * C:(h + 1) * C] * (zz * jax.nn.sigmoid(zz))).astype(BF16)


def _gdn(proj3, ba_rows, conv_w, gparams, norm_w):
    bsz, seq, _ = proj3.shape
    cps = GDN_CHUNKS_PER_STEP
    rows = cps * GDN_CHUNK
    prev_per_step = rows // PREV_ROWS
    return pl.pallas_call(
        _gdn_kernel,
        grid=(bsz, seq // rows),
        in_specs=[
            pl.BlockSpec((1, rows, 3 * GDN_WIDTH), lambda b, s: (b, s, 0)),
            pl.BlockSpec((1, PREV_ROWS, 3 * GDN_WIDTH),
                         lambda b, s: (b, jnp.maximum(s * prev_per_step - 1, 0), 0)),
            pl.BlockSpec((1, rows, GDN_WIDTH), lambda b, s: (b, s, 3)),
            pl.BlockSpec((1, cps, 2, GDN_HC), lambda b, s: (b, s, 0, 0)),
            pl.BlockSpec((GDN_CONV, 3 * GDN_WIDTH), lambda b, s: (0, 0)),
            pl.BlockSpec((2, GDN_HC), lambda b, s: (0, 0)),
            pl.BlockSpec((1, GDN_HEAD_DIM), lambda b, s: (0, 0)),
        ],
        out_specs=pl.BlockSpec((1, rows, GDN_WIDTH), lambda b, s: (b, s, 0)),
        out_shape=jax.ShapeDtypeStruct((bsz, seq, GDN_WIDTH), BF16),
        scratch_shapes=[pltpu.VMEM((GDN_HEADS, GDN_HEAD_DIM, GDN_HEAD_DIM), F32)],
        compiler_params=_cparams(2),
        name="gdn",
    )(proj3, proj3, proj3, ba_rows, conv_w, gparams, norm_w)


def _sb_kernel(q_ref, k_ref, v_ref, nw_ref, out_ref, acc_ref, later_ref):
    T, Dh = SB_TILE, SB_HEAD_DIM
    qb = pl.program_id(2)
    row = lax.broadcasted_iota(I32, (T, T), 0)
    col = lax.broadcasted_iota(I32, (T, T), 1)
    causal = col < row
    neg_suffix = jnp.where(row >= col, -1.0, 0.0).astype(BF16)
    neg_suffix2 = jnp.concatenate([neg_suffix, neg_suffix], axis=0)

    low = lax.broadcasted_iota(I32, (T, LANES), 1) < Dh
    q = q_ref[0] * jnp.asarray(Dh ** -0.5, BF16)
    zeros = jnp.zeros_like(q)
    q_heads = (jnp.where(low, q, zeros), jnp.where(low, zeros, q))

    def tiles(kbs, first_is_diagonal):
        blocks = []
        for n, kb in enumerate(kbs):
            start = pl.multiple_of(kb * T, T)
            blocks.append((k_ref[0, pl.ds(start, T), :], v_ref[0, pl.ds(start, T), :],
                           first_is_diagonal and n == 0))
        chains = [(i, k_blk, v_blk, on_diag) for i in range(2) for k_blk, v_blk, on_diag in blocks]
        zs = [lax.dot_general(q_heads[i], k_blk, (((1,), (1,)), ((), ())), preferred_element_type=F32)
              for i, k_blk, _, _ in chains]
        splits = []
        for (_, _, _, on_diag), z in zip(chains, zs):
            neg_log_remain = _softplus(z)
            if on_diag:
                neg_log_remain = jnp.where(causal, neg_log_remain, 0.0)
            splits.append(neg_log_remain.astype(BF16))
        incl_all = jnp.dot(jnp.concatenate(splits, axis=0), neg_suffix, preferred_element_type=F32)
        incl_after = [incl_all[n * T:(n + 1) * T] for n in range(len(chains))]
        later = [later_ref[0], later_ref[1]]
        weights = []
        for (i, _, _, on_diag), z, incl in zip(chains, zs, incl_after):
            w = jnp.exp(z + incl + later[i])
            if on_diag:
                w = jnp.where(causal, w, 0.0)
            weights.append(w.astype(BF16))
            later[i] = later[i] + incl[:, 0:1]
        acc = [acc_ref[0], acc_ref[1]]
        for (i, _, v_blk, _), w in zip(chains, weights):
            acc[i] = acc[i] + jnp.dot(w, v_blk, preferred_element_type=F32)
        for i in range(2):
            later_ref[i] = later[i]
            acc_ref[i] = acc[i]

    later_ref[...] = jnp.zeros_like(later_ref)
    acc_ref[...] = jnp.zeros_like(acc_ref)

    head = (qb % 4) + 1

    @pl.when(head == 1)
    def _():
        tiles([qb], True)

    @pl.when(head == 2)
    def _():
        tiles([qb, qb - 1], True)

    @pl.when(head == 3)
    def _():
        tiles([qb], True)
        tiles([qb - 1, qb - 2], False)

    @pl.when(head == 4)
    def _():
        tiles([qb, qb - 1, qb - 2, qb - 3], True)

    def quad(i, carry):
        kb = qb - head - 4 * i
        tiles([kb, kb - 1, kb - 2, kb - 3], False)
        return carry

    lax.fori_loop(0, (qb + 1 - head) // 4, quad, 0)

    o = jnp.where(low, acc_ref[0], acc_ref[1])
    hr = lax.broadcasted_iota(I32, (LANES, LANES), 0) // Dh
    hc = lax.broadcasted_iota(I32, (LANES, LANES), 1) // Dh
    avg = jnp.where(hr == hc, 1.0 / Dh, 0.0).astype(BF16)
    hi, lo = _split(o * o)
    ms = jnp.dot(jnp.concatenate([hi, lo], axis=1), jnp.concatenate([avg, avg], axis=0),
                 preferred_element_type=F32)
    out_ref[0] = (o * lax.rsqrt(ms + EPS) * nw_ref[...]).astype(BF16)


def _sb(proj3, norm_w2):
    bsz, seq, _ = proj3.shape
    tile = min(SB_TILE, seq)
    assert tile == SB_TILE and seq % tile == 0
    pairs = SB_WIDTH // LANES
    q_col = (3 * GDN_WIDTH + GDN_WIDTH) // LANES
    k_col = q_col + pairs
    v_col = k_col + pairs
    return pl.pallas_call(
        _sb_kernel,
        grid=(bsz, pairs, seq // tile),
        in_specs=[
            pl.BlockSpec((1, tile, LANES), lambda b, hp, qb: (b, qb, q_col + hp)),
            pl.BlockSpec((1, seq, LANES), lambda b, hp, qb: (b, 0, k_col + hp)),
            pl.BlockSpec((1, seq, LANES), lambda b, hp, qb: (b, 0, v_col + hp)),
            pl.BlockSpec((1, LANES), lambda b, hp, qb: (0, 0)),
        ],
        out_specs=pl.BlockSpec((1, tile, LANES), lambda b, hp, qb: (b, qb, hp)),
        out_shape=jax.ShapeDtypeStruct((bsz, seq, SB_WIDTH), BF16),
        scratch_shapes=[pltpu.VMEM((2, tile, LANES), F32),
                        pltpu.VMEM((2, tile, 1), F32)],
        compiler_params=_cparams(3),
        name="sb_attn",
    )(proj3, proj3, proj3, norm_w2)


def _post_mix_kernel(x_ref, g_ref, s_ref, wo_ref, ln_ref, wrt_ref, br_ref,
                     x1_ref, h2r_ref, route_ref, gcol_ref, cnt_ref, count_ref):
    tm = x_ref.shape[0]
    i = pl.program_id(0)

    @pl.when(i == 0)
    def _():
        count_ref[...] = jnp.zeros_like(count_ref)

    x1 = (x_ref[...]
          + jnp.dot(g_ref[...], wo_ref[0:GDN_WIDTH, :], preferred_element_type=F32)
          + jnp.dot(s_ref[...], wo_ref[GDN_WIDTH:, :], preferred_element_type=F32))
    x1_ref[...] = x1
    h2 = _rms(x1, ln_ref[...])
    h2r_ref[...] = _rows_to_tiles(h2)

    w_hi, w_lo = _split(wrt_ref[...])
    h_hi, h_lo = _split(h2)
    nt = (((1,), (1,)), ((), ()))
    logits = (lax.dot_general(w_hi, h_hi, nt, preferred_element_type=F32)
              + lax.dot_general(w_hi, h_lo, nt, preferred_element_type=F32)
              + lax.dot_general(w_lo, h_hi, nt, preferred_element_type=F32)) + br_ref[:, 0:1]

    g_logits = logits[N_EXPERTS:N_EXPERTS + N_GROUPS, :]
    g_max = jnp.max(g_logits, axis=0, keepdims=True)
    group_w = 1.0 / jnp.sum(jnp.exp(g_logits - g_max), axis=0, keepdims=True)
    sub8 = lax.broadcasted_iota(I32, (N_GROUPS, tm), 0)
    g_idx = jnp.min(jnp.where(g_logits == g_max, sub8, N_GROUPS), axis=0, keepdims=True)

    e_logits = logits[0:N_EXPERTS, :]
    erow = lax.broadcasted_iota(I32, (N_EXPERTS, tm), 0)
    neg_inf = -jnp.inf
    in_group = jnp.where((erow // EXPERTS_PER_GROUP) == g_idx, e_logits, neg_inf)
    m1 = jnp.max(in_group, axis=0, keepdims=True)
    idx1 = jnp.min(jnp.where(in_group == m1, erow, N_EXPERTS), axis=0, keepdims=True)
    rest = jnp.where(erow == idx1, neg_inf, in_group)
    m2 = jnp.max(rest, axis=0, keepdims=True)
    idx2 = jnp.min(jnp.where(rest == m2, erow, N_EXPERTS), axis=0, keepdims=True)
    r = jnp.exp(m2 - m1)
    w1 = 1.0 / (1.0 + r)
    gate1 = group_w * w1
    gate2 = group_w * (r * w1)

    oh1 = erow == idx1
    oh2 = erow == idx2
    picked = jnp.where(oh1 | oh2, 1.0, 0.0)
    tr = lax.broadcasted_iota(I32, (tm, tm), 0)
    tc = lax.broadcasted_iota(I32, (tm, tm), 1)
    before = jnp.where(tr < tc, 1.0, 0.0).astype(BF16)
    seen = jnp.dot(picked.astype(BF16), before, preferred_element_type=F32) + count_ref[:, 0:1]
    rank1 = jnp.sum(jnp.where(oh1, seen, 0.0), axis=0, keepdims=True)
    rank2 = jnp.sum(jnp.where(oh2, seen, 0.0), axis=0, keepdims=True)
    count_ref[...] = count_ref[...] + jnp.sum(picked, axis=1, keepdims=True)
    cnt_ref[...] = count_ref[...]

    sub = lax.broadcasted_iota(I32, (SUBLANES, tm), 0)
    rows = (idx1.astype(F32), idx2.astype(F32), rank1, rank2)
    route = jnp.zeros((SUBLANES, tm), F32)
    for n, val in enumerate(rows):
        route = jnp.where(sub == n, val, route)
    route_ref[...] = route

    sub_l = lax.broadcasted_iota(I32, (LANES, tm), 0)
    gates_t = jnp.where(sub_l == 0, gate1, jnp.where(sub_l == 1, gate2, 0.0))
    gcol_ref[...] = gates_t.T


def _post_mix(x2d, gdn_out, sb_out, w_out, ln, w_router_t, b_router):
    n_tok = x2d.shape[0]
    tm = min(POST_TM, n_tok)
    return pl.pallas_call(
        _post_mix_kernel,
        grid=(n_tok // tm,),
        in_specs=[
            pl.BlockSpec((tm, D_MODEL), lambda i: (i, 0)),
            pl.BlockSpec((tm, GDN_WIDTH), lambda i: (i, 0)),
            pl.BlockSpec((tm, SB_WIDTH), lambda i: (i, 0)),
            pl.BlockSpec((D_MODEL, D_MODEL), lambda i: (0, 0)),
            pl.BlockSpec((1, D_MODEL), lambda i: (0, 0)),
            pl.BlockSpec((LANES, D_MODEL), lambda i: (0, 0)),
            pl.BlockSpec((LANES, LANES), lambda i: (0, 0)),
        ],
        out_specs=[
            pl.BlockSpec((tm, D_MODEL), lambda i: (i, 0)),
            pl.BlockSpec((tm, ROW_TILES, LANES), lambda i: (i, 0, 0)),
            pl.BlockSpec((SUBLANES, tm), lambda i: (0, i)),
            pl.BlockSpec((tm, LANES), lambda i: (i, 0)),
            pl.BlockSpec((N_EXPERTS, LANES), lambda i: (0, 0)),
        ],
        out_shape=[
            jax.ShapeDtypeStruct((n_tok, D_MODEL), F32),
            jax.ShapeDtypeStruct((n_tok, ROW_TILES, LANES), F32),
            jax.ShapeDtypeStruct((SUBLANES, n_tok), F32),
            jax.ShapeDtypeStruct((n_tok, LANES), F32),
            jax.ShapeDtypeStruct((N_EXPERTS, LANES), F32),
        ],
        scratch_shapes=[pltpu.VMEM((N_EXPERTS, LANES), F32)],
        compiler_params=_cparams(1),
        name="post_mix",
    )(x2d, gdn_out, sb_out, w_out, ln, w_router_t, b_router)


def _dest_kernel(route_ref, start_ref, dest_ref):
    tm = route_ref.shape[1]
    erow = lax.broadcasted_iota(I32, (N_EXPERTS, tm), 0).astype(F32)
    starts = start_ref[:, 0:1]
    sub = lax.broadcasted_iota(I32, (SUBLANES, tm), 0)
    dest = jnp.zeros((SUBLANES, tm), F32)
    for n in range(2):
        expert = route_ref[n:n + 1, :]
        rank = route_ref[2 + n:3 + n, :]
        row = jnp.sum(jnp.where(erow == expert, starts, 0.0), axis=0, keepdims=True) + rank
        dest = jnp.where(sub == n, row, dest)
    dest_ref[...] = dest.astype(I32)


def _dest(route, padded_start):
    n_tok = route.shape[1]
    tm = min(DEST_TM, n_tok)
    starts = jnp.broadcast_to(padded_start.astype(F32)[:, None], (N_EXPERTS, LANES))
    return pl.pallas_call(
        _dest_kernel,
        grid=(n_tok // tm,),
        in_specs=[pl.BlockSpec((SUBLANES, tm), lambda i: (0, i)),
                  pl.BlockSpec((N_EXPERTS, LANES), lambda i: (0, 0))],
        out_specs=pl.BlockSpec((SUBLANES, tm), lambda i: (0, i)),
        out_shape=jax.ShapeDtypeStruct((SUBLANES, n_tok), I32),
        compiler_params=_cparams(1),
        name="moe_dest",
    )(route, starts)


def _dispatch_kernel(d1_ref, d2_ref, rows_ref, buf_in_hbm, buf_hbm, sem):
    del buf_in_hbm
    tm = rows_ref.shape[0]

    def issue(t, carry):
        pltpu.make_async_copy(rows_ref.at[t], buf_hbm.at[d1_ref[0, 0, t]], sem).start(priority=0)
        pltpu.make_async_copy(rows_ref.at[t], buf_hbm.at[d2_ref[0, 0, t]], sem).start(priority=1)
        return carry

    lax.fori_loop(0, tm, issue, 0, unroll=DMA_ISSUE_UNROLL)
    for _ in range(2):
        pltpu.make_async_copy(rows_ref, buf_hbm.at[pl.ds(0, tm)], sem).wait()


def _dispatch(dest1, dest2, h2r, n_rows):
    n_tok = h2r.shape[0]
    tm = min(DISP_TM, n_tok)
    n_steps = n_tok // tm
    d1 = dest1.reshape(n_steps, 1, tm)
    d2 = dest2.reshape(n_steps, 1, tm)
    smem_spec = pl.BlockSpec((1, 1, tm), lambda i: (i, 0, 0), memory_space=pltpu.SMEM)
    buf0 = jnp.zeros((n_rows, ROW_TILES, LANES), F32)
    return pl.pallas_call(
        _dispatch_kernel,
        grid=(n_steps,),
        in_specs=[smem_spec, smem_spec,
                  pl.BlockSpec((tm, ROW_TILES, LANES), lambda i: (i, 0, 0)),
                  pl.BlockSpec(memory_space=pl.ANY)],
        out_specs=pl.BlockSpec(memory_space=pl.ANY),
        out_shape=jax.ShapeDtypeStruct((n_rows, ROW_TILES, LANES), F32),
        scratch_shapes=[pltpu.SemaphoreType.DMA(())],
        input_output_aliases={3: 0},
        compiler_params=_cparams(1),
        name="moe_dispatch",
    )(d1, d2, h2r, buf0)


def _ffn_kernel(be_ref, x_ref, wg_ref, wu_ref, wd_ref, y_ref, wgu_bf, wd_bf):
    i = pl.program_id(0)

    @pl.when((i == 0) | (be_ref[i] != be_ref[jnp.maximum(i - 1, 0)]))
    def _():
        wgu_bf[:, :D_EXPERT] = wg_ref[0].astype(BF16)
        wgu_bf[:, D_EXPERT:] = wu_ref[0].astype(BF16)
        wd_bf[...] = wd_ref[0].astype(BF16)

    acc = jnp.dot(_tiles_to_rows(x_ref[...]).astype(BF16), wgu_bf[...], preferred_element_type=F32)
    gate, up = acc[:, :D_EXPERT], acc[:, D_EXPERT:]
    hidden = (gate * jax.nn.sigmoid(gate)) * up
    y_ref[...] = _rows_to_tiles(jnp.dot(hidden.astype(BF16), wd_bf[...], preferred_element_type=F32))


def _ffn(block_expert, buf, w_gate, w_up, w_down):
    n_rows = buf.shape[0]
    n_blocks = n_rows // MOE_BLOCK
    grid_spec = pltpu.PrefetchScalarGridSpec(
        num_scalar_prefetch=1,
        grid=(n_blocks,),
        in_specs=[
            pl.BlockSpec((MOE_BLOCK, ROW_TILES, LANES), lambda i, be: (i, 0, 0)),
            pl.BlockSpec((1, D_MODEL, D_EXPERT), lambda i, be: (be[i], 0, 0)),
            pl.BlockSpec((1, D_MODEL, D_EXPERT), lambda i, be: (be[i], 0, 0)),
            pl.BlockSpec((1, D_EXPERT, D_MODEL), lambda i, be: (be[i], 0, 0)),
        ],
        out_specs=pl.BlockSpec((MOE_BLOCK, ROW_TILES, LANES), lambda i, be: (i, 0, 0)),
        scratch_shapes=[pltpu.VMEM((D_MODEL, 2 * D_EXPERT), BF16),
                        pltpu.VMEM((D_EXPERT, D_MODEL), BF16)],
    )
    return pl.pallas_call(
        _ffn_kernel,
        grid_spec=grid_spec,
        out_shape=jax.ShapeDtypeStruct((n_rows, ROW_TILES, LANES), F32),
        compiler_params=_cparams(1),
        name="moe_ffn",
    )(block_expert, buf, w_gate, w_up, w_down)


def _final_kernel(d1_ref, d2_ref, d1_next_ref, d2_next_ref, ybuf_hbm, x1_ref, gcol_ref, p_ref, wpg_ref,
                  wpp_ref, lnp_ref, lnf_ref, out_ref, y1_ref, y2_ref, sem):
    tm = x1_ref.shape[0]
    i = pl.program_id(0)
    slot = i % 2

    def gather(idx1_ref, idx2_ref, dst_slot):
        def issue(t, carry):
            pltpu.make_async_copy(ybuf_hbm.at[idx1_ref[0, 0, t]], y1_ref.at[dst_slot, t],
                                  sem.at[dst_slot]).start(priority=0)
            pltpu.make_async_copy(ybuf_hbm.at[idx2_ref[0, 0, t]], y2_ref.at[dst_slot, t],
                                  sem.at[dst_slot]).start(priority=1)
            return carry
        lax.fori_loop(0, tm, issue, 0, unroll=DMA_ISSUE_UNROLL)

    @pl.when(i == 0)
    def _():
        gather(d1_ref, d2_ref, 0)

    @pl.when(i + 1 < pl.num_programs(0))
    def _():
        gather(d1_next_ref, d2_next_ref, 1 - slot)

    pltpu.make_async_copy(ybuf_hbm.at[pl.ds(0, tm)], y1_ref.at[slot], sem.at[slot]).wait()
    pltpu.make_async_copy(ybuf_hbm.at[pl.ds(0, tm)], y2_ref.at[slot], sem.at[slot]).wait()

    g1 = gcol_ref[:, 0:1]
    g2 = gcol_ref[:, 1:2]
    x2 = x1_ref[...] + (g1 * _tiles_to_rows(y1_ref[slot]) + g2 * _tiles_to_rows(y2_ref[slot]))
    hp = _rms(x2, lnp_ref[...]).astype(BF16)
    ple_gate = jax.nn.sigmoid(jnp.dot(hp, wpg_ref[...], preferred_element_type=F32))
    ple = jnp.dot(p_ref[...].astype(BF16), wpp_ref[...], preferred_element_type=F32)
    x3 = x2 + ple_gate * ple
    out_ref[...] = _rms(x3, lnf_ref[...])


def _final(dest1, dest2, ybuf, x1, gcol, p2d, w_ple_gate, w_ple_proj, ln_ple, ln_final):
    n_tok = x1.shape[0]
    tm = min(FINAL_TM, n_tok)
    n_steps = n_tok // tm
    d1 = dest1.reshape(n_steps, 1, tm)
    d2 = dest2.reshape(n_steps, 1, tm)
    smem_spec = pl.BlockSpec((1, 1, tm), lambda i: (i, 0, 0), memory_space=pltpu.SMEM)
    smem_next = pl.BlockSpec((1, 1, tm), lambda i: (jnp.minimum(i + 1, n_steps - 1), 0, 0),
                             memory_space=pltpu.SMEM)
    return pl.pallas_call(
        _final_kernel,
        grid=(n_steps,),
        in_specs=[
            smem_spec, smem_spec, smem_next, smem_next,
            pl.BlockSpec(memory_space=pl.ANY),
            pl.BlockSpec((tm, D_MODEL), lambda i: (i, 0)),
            pl.BlockSpec((tm, LANES), lambda i: (i, 0)),
            pl.BlockSpec((tm, PLE_DIM), lambda i: (i, 0)),
            pl.BlockSpec((D_MODEL, D_MODEL), lambda i: (0, 0)),
            pl.BlockSpec((PLE_DIM, D_MODEL), lambda i: (0, 0)),
            pl.BlockSpec((1, D_MODEL), lambda i: (0, 0)),
            pl.BlockSpec((1, D_MODEL), lambda i: (0, 0)),
        ],
        out_specs=pl.BlockSpec((tm, D_MODEL), lambda i: (i, 0)),
        out_shape=jax.ShapeDtypeStruct((n_tok, D_MODEL), F32),
        scratch_shapes=[pltpu.VMEM((2, tm, ROW_TILES, LANES), F32),
                        pltpu.VMEM((2, tm, ROW_TILES, LANES), F32),
                        pltpu.SemaphoreType.DMA((2,))],
        compiler_params=_cparams(1),
        name="moe_combine_ple_final",
    )(d1, d2, d1, d2, ybuf, x1, gcol, p2d, w_ple_gate, w_ple_proj, ln_ple, ln_final)


def _layer(x2d, p2d, bsz, seq, ln_mix, w_in, conv_w, a_log, dt_bias, gdn_norm_w, sb_norm_w, w_out,
           ln_moe, w_rg, b_rg, w_re, b_re, w_eg, w_eu, w_ed, ln_ple, w_ple_gate, w_ple_proj, ln_final):
    n_tok = bsz * seq
    C, H = GDN_CHUNK, GDN_HEADS
    o_z = 3 * GDN_WIDTH
    o_b = o_z + GDN_WIDTH
    o_q = o_b + 2 * H
    w_all = jnp.concatenate(
        [w_in[:, :o_b], w_in[:, o_q:], w_in[:, o_b:o_q],
         jnp.zeros((D_MODEL, LANES - 2 * H), F32)], axis=1).astype(BF16)
    proj, ba = _in_proj(x2d, ln_mix.reshape(1, D_MODEL), w_all)
    proj3 = proj.reshape(bsz, seq, PROJ_W)
    ba_rows = (ba[:, :2 * H].reshape(bsz, seq // C, C, 2, H)
               .transpose(0, 1, 3, 4, 2).reshape(bsz, seq // C, 2, GDN_HC))
    gparams = jnp.repeat(jnp.stack([a_log, dt_bias]).astype(F32), C, axis=1)

    gdn_out = _gdn(proj3, ba_rows, conv_w, gparams, gdn_norm_w.reshape(1, GDN_HEAD_DIM))
    sb_out = _sb(proj3, jnp.tile(sb_norm_w, LANES // SB_HEAD_DIM).reshape(1, LANES))

    w_router_t = jnp.zeros((LANES, D_MODEL), F32)
    w_router_t = w_router_t.at[:N_EXPERTS].set(w_re.T).at[N_EXPERTS:N_EXPERTS + N_GROUPS].set(w_rg.T)
    b_router = jnp.zeros((LANES,), F32).at[:N_EXPERTS].set(b_re).at[N_EXPERTS:N_EXPERTS + N_GROUPS].set(b_rg)
    b_router = jnp.broadcast_to(b_router[:, None], (LANES, LANES))
    x1, h2r, route, gcol, cnt = _post_mix(
        x2d, gdn_out.reshape(n_tok, GDN_WIDTH), sb_out.reshape(n_tok, SB_WIDTH),
        w_out.astype(BF16), ln_moe.reshape(1, D_MODEL), w_router_t, b_router)

    counts = cnt[:, 0].astype(I32)
    padded = (counts + MOE_BLOCK - 1) // MOE_BLOCK * MOE_BLOCK
    padded_end = jnp.cumsum(padded)
    padded_start = padded_end - padded
    dest = _dest(route, padded_start)
    dest1, dest2 = dest[0], dest[1]
    n_blocks = (2 * n_tok + MOE_BLOCK - 1) // MOE_BLOCK + N_EXPERTS
    block_row0 = jnp.arange(n_blocks, dtype=I32) * MOE_BLOCK
    block_expert = jnp.minimum(
        jnp.sum((padded_end[:, None] <= block_row0[None, :]).astype(I32), axis=0), N_EXPERTS - 1)

    buf = _dispatch(dest1, dest2, h2r, n_blocks * MOE_BLOCK)
    ybuf = _ffn(block_expert, buf, w_eg, w_eu, w_ed)
    return _final(dest1, dest2, ybuf, x1, gcol, p2d, w_ple_gate.astype(BF16), w_ple_proj.astype(BF16),
                  ln_ple.reshape(1, D_MODEL), ln_final.reshape(1, D_MODEL))


def kernel(x, p, ln_mix, w_in, conv_w, a_log, dt_bias, gdn_norm_w, sb_norm_w, w_out, ln_moe,
           w_router_group, b_router_group, w_router_expert, b_router_expert, w_expert_gate,
           w_expert_up, w_expert_down, ln_ple, w_ple_gate, w_ple_proj, ln_final):
    bsz, seq, _ = x.shape
    depth = p.shape[0]
    assert depth == 1, "the final RMSNorm is fused into the layer's last kernel"
    x2d = x.reshape(bsz * seq, D_MODEL)
    out = _layer(x2d, p[0].reshape(bsz * seq, PLE_DIM), bsz, seq, ln_mix[0], w_in[0], conv_w[0],
                 a_log[0], dt_bias[0], gdn_norm_w[0], sb_norm_w[0], w_out[0], ln_moe[0],
                 w_router_group[0], b_router_group[0], w_router_expert[0], b_router_expert[0],
                 w_expert_gate[0], w_expert_up[0], w_expert_down[0], ln_ple[0], w_ple_gate[0],
                 w_ple_proj[0], ln_final)
    return out.reshape(bsz, seq, D_MODEL)
```
